```python
import jax, jax.numpy as jnp
from jax import lax
import numpy as np

D_MODEL = 1024
BATCH = 4
SEQ = 8192
DEPTH = 2

N_MIXERS = 2
N_A = (DEPTH + 1) // 2
N_B = DEPTH // 2
N_META = 16
EXPAND = 2
E_CONV = EXPAND * D_MODEL
CONV_WIDTH = 3
E_MLSTM = EXPAND * D_MODEL
N_HEADS = 4
DV = E_MLSTM // N_HEADS
DK = DV // 2
QK = N_HEADS * DK
CHUNK = 64
RMS_EPS = 1e-6

kernel_name = "hybrid_shortconv_mlstm_interleaved"


def _rmsnorm(x, w):
    xf = x.astype(jnp.float32)
    y = xf * lax.rsqrt(jnp.mean(xf * xf, axis=-1, keepdims=True) + RMS_EPS)
    return (y * w.astype(jnp.float32)).astype(x.dtype)


def _short_conv_mixer(u, w_in, w_conv, w_out):
    proj = u @ w_in
    b_gate, c_gate, xin, z = jnp.split(proj, 4, axis=-1)
    y = lax.conv_general_dilated(
        c_gate * xin, w_conv[:, None, :].astype(u.dtype),
        window_strides=(1,), padding=[(CONV_WIDTH - 1, 0)],
        dimension_numbers=("NWC", "WIO", "NWC"), feature_group_count=E_CONV)
    return (jax.nn.silu(z) * b_gate * y) @ w_out


def _mlstm_chunk(carry, inp):
    c_state, n_state, m_state = carry
    q, k, v, logi, logf = inp
    L = q.shape[-2]
    b = jnp.cumsum(logf, axis=-1)
    causal = jnp.tril(jnp.ones((L, L), dtype=bool))
    log_d = jnp.where(causal, b[..., :, None] - b[..., None, :] + logi[..., None, :], -jnp.inf)
    log_inter = b + m_state[..., None]
    m_row = jnp.maximum(log_inter, jnp.max(log_d, axis=-1))
    d = jnp.exp(log_d - m_row[..., None])
    inter = jnp.exp(log_inter - m_row)
    s = jnp.einsum("bhtd,bhsd->bhts", q, k) * d
    num = jnp.einsum("bhts,bhsv->bhtv", s, v) + inter[..., None] * jnp.einsum("bhtd,bhdv->bhtv", q, c_state)
    den = jnp.sum(s, axis=-1) + inter * jnp.einsum("bhtd,bhd->bht", q, n_state)
    h = num / jnp.maximum(jnp.abs(den), jnp.exp(-m_row))[..., None]
    log_w = b[..., -1:] - b + logi
    m_new = jnp.maximum(b[..., -1] + m_state, jnp.max(log_w, axis=-1))
    decay = jnp.exp(b[..., -1] + m_state - m_new)
    w = jnp.exp(log_w - m_new[..., None])
    c_new = decay[..., None, None] * c_state + jnp.einsum("bhs,bhsd,bhsv->bhdv", w, k, v)
    n_new = decay[..., None] * n_state + jnp.einsum("bhs,bhsd->bhd", w, k)
    return (c_new, n_new, m_new), h


def _mlstm_mixer(u, w_in, gate_b, head_norm_w, w_out):
    bsz, T, _ = u.shape
    proj = u @ w_in
    q, k, v, o, z, g = jnp.split(
        proj, [QK, 2 * QK, 2 * QK + E_MLSTM, 2 * QK + 2 * E_MLSTM, 2 * QK + 3 * E_MLSTM], axis=-1)
    g = g.astype(jnp.float32) + gate_b.astype(jnp.float32)
    logi = jnp.transpose(g[..., :N_HEADS], (0, 2, 1))
    logf = jnp.transpose(jax.nn.log_sigmoid(g[..., N_HEADS:]), (0, 2, 1))

    def heads(a, dh):
        return a.astype(jnp.float32).reshape(bsz, T, N_HEADS, dh).transpose(0, 2, 1, 3)

    q = heads(q, DK) * (DK ** -0.5)
    k = heads(k, DK)
    v = heads(v, DV)
    seqs = (q, k, v, logi, logf)

    def chunks(a):
        real = a[:, :, N_META:]
        nc = real.shape[2] // CHUNK
        return jnp.moveaxis(real.reshape(bsz, N_HEADS, nc, CHUNK, *real.shape[3:]), 2, 0)

    carry0 = (jnp.zeros((bsz, N_HEADS, DK, DV), jnp.float32),
              jnp.zeros((bsz, N_HEADS, DK), jnp.float32),
              jnp.zeros((bsz, N_HEADS), jnp.float32))
    carry, h_meta = _mlstm_chunk(carry0, tuple(a[:, :, :N_META] for a in seqs))
    _, h_real = lax.scan(_mlstm_chunk, carry, tuple(chunks(a) for a in seqs))
    h_real = jnp.moveaxis(h_real, 0, 2).reshape(bsz, N_HEADS, T - N_META, DV)
    h = jnp.concatenate([h_meta, h_real], axis=2).transpose(0, 2, 1, 3)
    h = h * lax.rsqrt(jnp.mean(h * h, axis=-1, keepdims=True) + RMS_EPS)
    h = h * head_norm_w.astype(jnp.float32).reshape(N_HEADS, DV)
    h = h.reshape(bsz, T, E_MLSTM).astype(u.dtype) * jax.nn.sigmoid(o) * jax.nn.silu(z)
    return h @ w_out


def setup_inputs(seed: int = 0) -> dict:
    key = jax.random.key(seed)
    ks = jax.random.split(key, 12)
    f32 = jnp.float32
    x = jax.random.normal(ks[0], (BATCH, SEQ, D_MODEL), f32)
    meta_tokens = jax.random.normal(ks[1], (N_META, D_MODEL), f32)
    norm_w = 1.0 + 0.05 * jax.random.normal(ks[2], (DEPTH, D_MODEL), f32)
    conv_in_w = jax.random.normal(ks[3], (N_A, D_MODEL, 4 * E_CONV), f32) * D_MODEL ** -0.5
    conv_w = jax.random.normal(ks[4], (N_A, CONV_WIDTH, E_CONV), f32) * CONV_WIDTH ** -0.5
    conv_out_w = jax.random.normal(ks[5], (N_A, E_CONV, D_MODEL), f32) * E_CONV ** -0.5
    n_in = 2 * QK + 3 * E_MLSTM + 2 * N_HEADS
    mlstm_in_w = jax.random.normal(ks[6], (N_B, D_MODEL, n_in), f32) * D_MODEL ** -0.5
    i_bias = 0.1 * jax.random.normal(ks[7], (N_B, N_HEADS), f32)
    f_bias = jnp.linspace(3.0, 6.0, N_HEADS, dtype=f32)[None, :] + 0.1 * jax.random.normal(ks[8], (N_B, N_HEADS), f32)
    mlstm_gate_b = jnp.concatenate([i_bias, f_bias], axis=-1)
    mlstm_head_norm_w = 1.0 + 0.05 * jax.random.normal(ks[9], (N_B, E_MLSTM), f32)
    mlstm_out_w = jax.random.normal(ks[10], (N_B, E_MLSTM, D_MODEL), f32) * E_MLSTM ** -0.5
    final_norm_w = 1.0 + 0.05 * jax.random.normal(ks[11], (D_MODEL,), f32)
    return {"x": x, "meta_tokens": meta_tokens, "norm_w": norm_w, "conv_in_w": conv_in_w,
            "conv_w": conv_w, "conv_out_w": conv_out_w, "mlstm_in_w": mlstm_in_w,
            "mlstm_gate_b": mlstm_gate_b, "mlstm_head_norm_w": mlstm_head_norm_w,
            "mlstm_out_w": mlstm_out_w, "final_norm_w": final_norm_w}


def reference(x, meta_tokens, norm_w, conv_in_w, conv_w, conv_out_w, mlstm_in_w,
              mlstm_gate_b, mlstm_head_norm_w, mlstm_out_w, final_norm_w):
    bsz = x.shape[0]
    meta = jnp.broadcast_to(meta_tokens.astype(x.dtype)[None], (bsz, N_META, D_MODEL))
    h = jnp.concatenate([meta, x], axis=1)
    for i in range(DEPTH):
        u = _rmsnorm(h, norm_w[i])
        j = i // N_MIXERS
        if i % N_MIXERS == 0:
            h = h + _short_conv_mixer(u, conv_in_w[j], conv_w[j], conv_out_w[j])
        else:
            h = h + _mlstm_mixer(u, mlstm_in_w[j], mlstm_gate_b[j], mlstm_head_norm_w[j], mlstm_out_w[j])
    h = _rmsnorm(h, final_norm_w)
    return h[:, N_META:]
```

```python
import functools

import jax
import jax.numpy as jnp
from jax import lax
from jax.experimental import pallas as pl
from jax.experimental.pallas import tpu as pltpu

D_MODEL = 1024
N_META = 16
E_CONV = 2048
CONV_WIDTH = 3
E_MLSTM = 2048
N_HEADS = 4
DV = E_MLSTM // N_HEADS
DK = DV // 2
QK = N_HEADS * DK
RMS_EPS = 1e-6

LANES = 128
SUBLANES = 8
DV_EXT = DV + LANES
HEAD_W = DK + 3 * DV

CONV_TM = 512
CONV_TE = 512
MLSTM_L = 256
VMEM_LIMIT = 56 * 1024 * 1024

F32 = jnp.float32
BF16 = jnp.bfloat16


def _rms_scale(x, w_row):
    ms = jnp.mean(x * x, axis=-1, keepdims=True)
    return x * lax.rsqrt(ms + RMS_EPS) * w_row


def _log_sigmoid(x):
    return jnp.minimum(x, 0.0) - jnp.log1p(jnp.exp(-jnp.abs(x)))


def _resident(shape):
    nd = len(shape)
    return pl.BlockSpec(shape, lambda *_: (0,) * nd, pipeline_mode=pl.Buffered(1))


def _conv_layer_kernel(x_ref, halo_in_ref, nw_ref, w_in_ref, cw_ref, w_out_ref, *rest,
                       n_chunks, te, emit_tail):
    if emit_tail:
        h_ref, tail_ref, halo_ref = rest
    else:
        h_ref, halo_ref = rest

    @pl.when(pl.program_id(1) == 0)
    def _():
        halo_ref[...] = halo_in_ref[...]

    x = x_ref[0]
    tm = x.shape[0]
    u = _rms_scale(x, nw_ref[...]).astype(BF16)
    row = lax.broadcasted_iota(jnp.int32, (tm, te), 0)
    acc = x
    for e in range(n_chunks):
        p = jnp.dot(u, w_in_ref[e], preferred_element_type=F32)
        b_gate = p[:, 0 * te:1 * te]
        c_gate = p[:, 1 * te:2 * te]
        xin = p[:, 2 * te:3 * te]
        z = p[:, 3 * te:4 * te]
        cx = c_gate * xin
        halo = halo_ref[e]
        prev1 = jnp.where(row == 0, halo[7:8, :], pltpu.roll(cx, 1, 0))
        prev2 = jnp.where(row == 0, halo[6:7, :],
                          jnp.where(row == 1, halo[7:8, :], pltpu.roll(cx, 2, 0)))
        cw = cw_ref[e]
        y = cw[0:1, :] * prev2 + cw[1:2, :] * prev1 + cw[2:3, :] * cx
        halo_ref[e] = cx[tm - SUBLANES:, :]
        g = (z * jax.nn.sigmoid(z)) * b_gate * y
        acc = acc + jnp.dot(g.astype(BF16), w_out_ref[e], preferred_element_type=F32)
    h_ref[0] = acc
    if emit_tail:
        tail_ref[...] = halo_ref[...]


def _conv_layer(x, halo_in, nw, w_in, cw, w_out, *, tm, emit_tail):
    bsz, t_len, d = x.shape
    n_chunks, _, te4 = w_in.shape
    te = te4 // 4
    out_shape = [jax.ShapeDtypeStruct((bsz, t_len, d), F32)]
    out_specs = [pl.BlockSpec((1, tm, d), lambda b, t: (b, t, 0))]
    if emit_tail:
        out_shape.append(jax.ShapeDtypeStruct((n_chunks, SUBLANES, te), F32))
        out_specs.append(pl.BlockSpec((n_chunks, SUBLANES, te), lambda b, t: (0, 0, 0)))
    return pl.pallas_call(
        functools.partial(_conv_layer_kernel, n_chunks=n_chunks, te=te, emit_tail=emit_tail),
        grid=(bsz, t_len // tm),
        in_specs=[
            pl.BlockSpec((1, tm, d), lambda b, t: (b, t, 0)),
            _resident(halo_in.shape),
            _resident(nw.shape),
            _resident(w_in.shape),
            _resident(cw.shape),
            _resident(w_out.shape),
        ],
        out_specs=out_specs,
        out_shape=out_shape,
        scratch_shapes=[pltpu.VMEM((n_chunks, SUBLANES, te), F32)],
        compiler_params=pltpu.CompilerParams(
            dimension_semantics=("arbitrary", "arbitrary"), vmem_limit_bytes=VMEM_LIMIT),
        name="conv_layer_meta" if emit_tail else "conv_layer",
    )(x, halo_in, nw, w_in, cw, w_out)


def _gate_columns(u, wg_ref, gb_ref):
    n = u.shape[0]
    g = jnp.dot(u, wg_ref[...], preferred_element_type=F32) + gb_ref[...]
    r = lax.broadcasted_iota(jnp.int32, (n, n), 0)
    c = lax.broadcasted_iota(jnp.int32, (n, n), 1)
    tril = (r >= c).astype(F32)
    b = jnp.dot(tril, _log_sigmoid(g), preferred_element_type=F32, precision=lax.Precision.HIGHEST)
    return g, g.T, b, b.T


def _state_update(c_ext, m_prev, kt, v_ext, li_r, b_r, b_last):
    logw = b_last - b_r + li_r
    m_new = jnp.maximum(b_last + m_prev, jnp.max(logw, axis=1, keepdims=True))
    decay = jnp.exp(b_last + m_prev - m_new)
    w_r = jnp.exp(logw - m_new)
    kw = (kt * w_r).astype(BF16)
    return decay * c_ext + jnp.dot(kw, v_ext, preferred_element_type=F32), m_new


def _ones_block(n):
    lane = lax.broadcasted_iota(jnp.int32, (n, LANES), 1)
    return (lane == 0).astype(BF16)


def _mlstm_layer_kernel(h_ref, c0_ref, m0_ref, nw_ref, wh_ref, wkt_ref, wg_ref, gb_ref, hw_ref,
                        wo_ref, fw_ref, o_ref, c_ref, m_ref):
    @pl.when(pl.program_id(1) == 0)
    def _():
        c_ref[...] = c0_ref[...]
        m_ref[...] = m0_ref[...]

    h1 = h_ref[0]
    n = h1.shape[0]
    u = _rms_scale(h1, nw_ref[...]).astype(BF16)
    g, g_t, b, b_t = _gate_columns(u, wg_ref, gb_ref)
    r = lax.broadcasted_iota(jnp.int32, (n, n), 0)
    c = lax.broadcasted_iota(jnp.int32, (n, n), 1)
    causal = r >= c
    ones = _ones_block(n)
    acc = h1
    for hd in range(N_HEADS):
        p = jnp.dot(u, wh_ref[hd], preferred_element_type=F32)
        q = (p[:, :DK] * (DK ** -0.5)).astype(BF16)
        v_ext = jnp.concatenate([p[:, DK:DK + DV].astype(BF16), ones], axis=1)
        o_pre = p[:, DK + DV:DK + 2 * DV]
        z_pre = p[:, DK + 2 * DV:]
        kt = lax.dot_general(wkt_ref[hd], u, (((1,), (1,)), ((), ())),
                             preferred_element_type=F32)
        li_r = g_t[hd:hd + 1, :]
        b_r = b_t[N_HEADS + hd:N_HEADS + hd + 1, :]
        b_c = b[:, N_HEADS + hd:N_HEADS + hd + 1]
        m_prev = m_ref[hd][0:1, 0:1]
        c_ext = c_ref[hd]

        log_d = jnp.where(causal, b_c - b_r + li_r, -jnp.inf)
        log_inter = b_c + m_prev
        m_row = jnp.maximum(log_inter, jnp.max(log_d, axis=1, keepdims=True))
        d = jnp.exp(log_d - m_row)
        inter = jnp.exp(log_inter - m_row)
        s = jnp.dot(q, kt.astype(BF16), preferred_element_type=F32) * d
        intra = jnp.dot(s.astype(BF16), v_ext, preferred_element_type=F32)
        cross = jnp.dot(q, c_ext.astype(BF16), preferred_element_type=F32)
        num = intra[:, :DV] + inter * cross[:, :DV]
        den = intra[:, DV:DV + 1] + inter * cross[:, DV:DV + 1]
        hh = num * (1.0 / jnp.maximum(jnp.abs(den), jnp.exp(-m_row)))

        c_new, m_new = _state_update(c_ext, m_prev, kt, v_ext, li_r, b_r, b_c[n - 1:n, :])
        c_ref[hd] = c_new
        m_ref[hd] = jnp.broadcast_to(m_new, (SUBLANES, LANES))

        hn = hh * lax.rsqrt(jnp.mean(hh * hh, axis=-1, keepdims=True) + RMS_EPS)
        hn = hn * hw_ref[:, hd * DV:(hd + 1) * DV]
        gated = hn * jax.nn.sigmoid(o_pre) * (z_pre * jax.nn.sigmoid(z_pre))
        acc = acc + jnp.dot(gated.astype(BF16), wo_ref[hd], preferred_element_type=F32)
    o_ref[0] = _rms_scale(acc, fw_ref[...])


def _mlstm_layer(h1, c0, m0, nw, wh, wkt, wg, gb, hw, wo, fw):
    bsz, t_len, d = h1.shape
    n_l = MLSTM_L
    return pl.pallas_call(
        _mlstm_layer_kernel,
        grid=(bsz, t_len // n_l),
        in_specs=[pl.BlockSpec((1, n_l, d), lambda b, t: (b, t, 0))]
        + [_resident(a.shape) for a in (c0, m0, nw, wh, wkt, wg, gb, hw, wo, fw)],
        out_specs=pl.BlockSpec((1, n_l, d), lambda b, t: (b, t, 0)),
        out_shape=jax.ShapeDtypeStruct((bsz, t_len, d), F32),
        scratch_shapes=[pltpu.VMEM((N_HEADS, DK, DV_EXT), F32),
                        pltpu.VMEM((N_HEADS, SUBLANES, LANES), F32)],
        compiler_params=pltpu.CompilerParams(
            dimension_semantics=("arbitrary", "arbitrary"), vmem_limit_bytes=VMEM_LIMIT),
        name="mlstm_layer",
    )(h1, c0, m0, nw, wh, wkt, wg, gb, hw, wo, fw)


def _meta_state_kernel(h_ref, nw_ref, wh_ref, wkt_ref, wg_ref, gb_ref, c_ref, m_ref):
    h1 = h_ref[...]
    n = h1.shape[0]
    u = _rms_scale(h1, nw_ref[...]).astype(BF16)
    g, g_t, b, b_t = _gate_columns(u, wg_ref, gb_ref)
    ones = _ones_block(n)
    for hd in range(N_HEADS):
        v = jnp.dot(u, wh_ref[hd, :, DK:DK + DV], preferred_element_type=F32)
        v_ext = jnp.concatenate([v.astype(BF16), ones], axis=1)
        kt = lax.dot_general(wkt_ref[hd], u, (((1,), (1,)), ((), ())), preferred_element_type=F32)
        b_c = b[:, N_HEADS + hd:N_HEADS + hd + 1]
        c_new, m_new = _state_update(
            jnp.zeros((DK, DV_EXT), F32), jnp.zeros((1, 1), F32), kt, v_ext,
            g_t[hd:hd + 1, :], b_t[N_HEADS + hd:N_HEADS + hd + 1, :], b_c[n - 1:n, :])
        c_ref[hd] = c_new
        m_ref[hd] = jnp.broadcast_to(m_new, (SUBLANES, LANES))


def _meta_state(h1_meta, nw, wh, wkt, wg, gb):
    args = (h1_meta, nw, wh, wkt, wg, gb)
    return pl.pallas_call(
        _meta_state_kernel,
        grid=(1,),
        in_specs=[_resident(a.shape) for a in args],
        out_specs=[pl.BlockSpec((N_HEADS, DK, DV_EXT), lambda i: (0, 0, 0)),
                   pl.BlockSpec((N_HEADS, SUBLANES, LANES), lambda i: (0, 0, 0))],
        out_shape=[jax.ShapeDtypeStruct((N_HEADS, DK, DV_EXT), F32),
                   jax.ShapeDtypeStruct((N_HEADS, SUBLANES, LANES), F32)],
        compiler_params=pltpu.CompilerParams(
            dimension_semantics=("arbitrary",), vmem_limit_bytes=VMEM_LIMIT),
        name="mlstm_meta_state",
    )(*args)


def kernel(x, meta_tokens, norm_w, conv_in_w, conv_w, conv_out_w, mlstm_in_w, mlstm_gate_b,
           mlstm_head_norm_w, mlstm_out_w, final_norm_w):
    te = CONV_TE
    n_chunks = E_CONV // te

    w_in = conv_in_w[0].reshape(D_MODEL, 4, n_chunks, te).transpose(2, 0, 1, 3)
    w_in = w_in.reshape(n_chunks, D_MODEL, 4 * te).astype(BF16)
    cw = conv_w[0].reshape(CONV_WIDTH, n_chunks, te).transpose(1, 0, 2)
    w_out = conv_out_w[0].reshape(n_chunks, te, D_MODEL).astype(BF16)

    mw = mlstm_in_w[0]
    wq = mw[:, :QK].reshape(D_MODEL, N_HEADS, DK)
    wk = mw[:, QK:2 * QK].reshape(D_MODEL, N_HEADS, DK)
    wv, wo_gate, wz = (mw[:, 2 * QK + i * E_MLSTM:2 * QK + (i + 1) * E_MLSTM]
                       .reshape(D_MODEL, N_HEADS, DV) for i in range(3))
    wh = jnp.concatenate([wq, wv, wo_gate, wz], axis=2).transpose(1, 0, 2).astype(BF16)
    wkt = wk.transpose(1, 2, 0).astype(BF16)
    n_gate = 2 * N_HEADS
    wg = jnp.pad(mw[:, 2 * QK + 3 * E_MLSTM:], ((0, 0), (0, LANES - n_gate))).astype(BF16)
    gb = jnp.pad(mlstm_gate_b[0], (0, LANES - n_gate)).reshape(1, LANES)
    hw = mlstm_head_norm_w[0].reshape(1, E_MLSTM)
    wo = mlstm_out_w[0].reshape(N_HEADS, DV, D_MODEL).astype(BF16)
    nw0 = norm_w[0].reshape(1, D_MODEL)
    nw1 = norm_w[1].reshape(1, D_MODEL)
    fw = final_norm_w.reshape(1, D_MODEL)

    zero_halo = jnp.zeros((n_chunks, SUBLANES, te), F32)
    h1_meta, meta_tail = _conv_layer(meta_tokens[None], zero_halo, nw0, w_in, cw, w_out,
                                     tm=N_META, emit_tail=True)
    c0, m0 = _meta_state(h1_meta[0], nw1, wh, wkt, wg, gb)

    (h1,) = _conv_layer(x, meta_tail, nw0, w_in, cw, w_out, tm=CONV_TM, emit_tail=False)
    return _mlstm_layer(h1, c0, m0, nw1, wh, wkt, wg, gb, hw, wo, fw)
```

```python
import functools

import jax
import jax.numpy as jnp
from jax import lax
from jax.experimental import pallas as pl
from jax.experimental.pallas import tpu as pltpu

D_MODEL = 1024
N_META = 16
E_CONV = 2048
CONV_WIDTH = 3
E_MLSTM = 2048
N_HEADS = 4
DV = E_MLSTM // N_HEADS
DK = DV // 2
QK = N_HEADS * DK
N_GATES = 2 * N_HEADS
RMS_EPS = 1e-6

LANES = 128
SUBLANES = 8
DV_EXT = DV + LANES
V_OFF = 2 * QK
O_OFF = V_OFF + E_MLSTM
Z_OFF = O_OFF + E_MLSTM
MLSTM_PROJ_W = Z_OFF + E_MLSTM

CONV_TM = 512
CONV_TE = 512
MLSTM_L = 256
VMEM_LIMIT = 56 * 1024 * 1024

F32 = jnp.float32
BF16 = jnp.bfloat16


def _rms_scale(x, w_row):
    ms = jnp.mean(x * x, axis=-1, keepdims=True)
    return x * lax.rsqrt(ms + RMS_EPS) * w_row


def _log_sigmoid(x):
    return jnp.minimum(x, 0.0) - jnp.log1p(jnp.exp(-jnp.abs(x)))


def _resident(shape):
    nd = len(shape)
    return pl.BlockSpec(shape, lambda *_: (0,) * nd, pipeline_mode=pl.Buffered(1))


def _conv_layer_kernel(x_ref, halo_in_ref, nw_ref, w_in_ref, cw_ref, w_out_ref, *rest,
                       te, emit_tail):
    if emit_tail:
        h_ref, tail_ref, halo_ref = rest
    else:
        h_ref, halo_ref = rest

    @pl.when(pl.program_id(1) == 0)
    def _():
        halo_ref[...] = halo_in_ref[...]

    x = x_ref[0]
    tm = x.shape[0]
    u = _rms_scale(x, nw_ref[...]).astype(BF16)
    row = lax.broadcasted_iota(jnp.int32, (tm, te), 0)
    acc = x
    for e in range(E_CONV // te):
        lo = e * te
        b_gate, c_gate, xin, z = (
            jnp.dot(u, w_in_ref[:, k * E_CONV + lo:k * E_CONV + lo + te], preferred_element_type=F32)
            for k in range(4))
        cx = c_gate * xin
        halo = halo_ref[:, lo:lo + te]
        prev1 = jnp.where(row == 0, halo[7:8, :], pltpu.roll(cx, 1, 0))
        prev2 = jnp.where(row == 0, halo[6:7, :],
                          jnp.where(row == 1, halo[7:8, :], pltpu.roll(cx, 2, 0)))
        cw = cw_ref[:, lo:lo + te]
        y = cw[0:1, :] * prev2 + cw[1:2, :] * prev1 + cw[2:3, :] * cx
        halo_ref[:, lo:lo + te] = cx[tm - SUBLANES:, :]
        g = (z * jax.nn.sigmoid(z)) * b_gate * y
        acc = acc + jnp.dot(g.astype(BF16), w_out_ref[lo:lo + te, :], preferred_element_type=F32)
    h_ref[0] = acc
    if emit_tail:
        tail_ref[...] = halo_ref[...]


def _conv_layer(x, halo_in, nw, w_in, cw, w_out, *, tm, emit_tail):
    bsz, t_len, d = x.shape
    out_shape = [jax.ShapeDtypeStruct((bsz, t_len, d), F32)]
    out_specs = [pl.BlockSpec((1, tm, d), lambda b, t: (b, t, 0))]
    if emit_tail:
        out_shape.append(jax.ShapeDtypeStruct((SUBLANES, E_CONV), F32))
        out_specs.append(pl.BlockSpec((SUBLANES, E_CONV), lambda b, t: (0, 0)))
    return pl.pallas_call(
        functools.partial(_conv_layer_kernel, te=min(CONV_TE, E_CONV), emit_tail=emit_tail),
        grid=(bsz, t_len // tm),
        in_specs=[
            pl.BlockSpec((1, tm, d), lambda b, t: (b, t, 0)),
            _resident(halo_in.shape),
            _resident(nw.shape),
            _resident(w_in.shape),
            _resident(cw.shape),
            _resident(w_out.shape),
        ],
        out_specs=out_specs,
        out_shape=out_shape,
        scratch_shapes=[pltpu.VMEM((SUBLANES, E_CONV), F32)],
        compiler_params=pltpu.CompilerParams(
            dimension_semantics=("arbitrary", "arbitrary"), vmem_limit_bytes=VMEM_LIMIT),
        name="conv_layer_meta" if emit_tail else "conv_layer",
    )(x, halo_in, nw, w_in, cw, w_out)


def _gate_columns(u, wg_ref, gb_ref):
    n = u.shape[0]
    g = jnp.dot(u, wg_ref[...], preferred_element_type=F32) + gb_ref[...]
    lf = _log_sigmoid(g)
    hi = lf.astype(BF16).astype(F32)
    mid = (lf - hi).astype(BF16).astype(F32)
    low = lf - hi - mid
    lane = lax.broadcasted_iota(jnp.int32, (n, LANES), 1)
    terms = jnp.where(lane < N_GATES, hi, jnp.where(lane < 2 * N_GATES, mid, low)).astype(BF16)
    r = lax.broadcasted_iota(jnp.int32, (n, n), 0)
    c = lax.broadcasted_iota(jnp.int32, (n, n), 1)
    tril = (r >= c).astype(BF16)
    part = jnp.dot(tril, terms, preferred_element_type=F32)
    b = part + pltpu.roll(part, LANES - N_GATES, 1) + pltpu.roll(part, LANES - 2 * N_GATES, 1)
    return g, g.T, b, b.T


def _state_update(c_ext, m_prev, kt, v_ext, li_r, b_r, b_last):
    logw = b_last - b_r + li_r
    m_new = jnp.maximum(b_last + m_prev, jnp.max(logw, axis=1, keepdims=True))
    decay = jnp.exp(b_last + m_prev - m_new)
    w_r = jnp.exp(logw - m_new)
    kw = (kt * w_r).astype(BF16)
    return decay * c_ext + jnp.dot(kw, v_ext, preferred_element_type=F32), m_new


def _ones_block(n):
    lane = lax.broadcasted_iota(jnp.int32, (n, LANES), 1)
    return (lane == 0).astype(BF16)


def _k_transposed(wkt_ref, u, hd):
    return lax.dot_general(wkt_ref[hd], u, (((1,), (1,)), ((), ())), preferred_element_type=F32)


def _mlstm_layer_kernel(h_ref, c0_ref, m0_ref, nw_ref, w_ref, wkt_ref, wg_ref, gb_ref, hw_ref,
                        wo_ref, fw_ref, o_ref, c_ref, m_ref):
    @pl.when(pl.program_id(1) == 0)
    def _():
        c_ref[...] = c0_ref[...]
        m_ref[...] = m0_ref[...]

    h1 = h_ref[0]
    n = h1.shape[0]
    u = _rms_scale(h1, nw_ref[...]).astype(BF16)
    g, g_t, b, b_t = _gate_columns(u, wg_ref, gb_ref)
    r = lax.broadcasted_iota(jnp.int32, (n, n), 0)
    c = lax.broadcasted_iota(jnp.int32, (n, n), 1)
    causal = r >= c
    ones = _ones_block(n)
    heads = range(N_HEADS)

    def proj(off, width, hd):
        return jnp.dot(u, w_ref[:, off + hd * width:off + (hd + 1) * width], preferred_element_type=F32)

    q = [(proj(0, DK, hd) * (DK ** -0.5)).astype(BF16) for hd in heads]
    kt = [_k_transposed(wkt_ref, u, hd) for hd in heads]
    v_ext = [jnp.concatenate([proj(V_OFF, DV, hd).astype(BF16), ones], axis=1) for hd in heads]

    li_r = [g_t[hd:hd + 1, :] for hd in heads]
    b_r = [b_t[N_HEADS + hd:N_HEADS + hd + 1, :] for hd in heads]
    b_c = [b[:, N_HEADS + hd:N_HEADS + hd + 1] for hd in heads]
    m_prev = [m_ref[hd][0:1, 0:1] for hd in heads]
    c_ext = [c_ref[hd] for hd in heads]

    m_row, inter, s = [], [], []
    for hd in heads:
        log_d = jnp.where(causal, b_c[hd] - b_r[hd] + li_r[hd], -jnp.inf)
        log_inter = b_c[hd] + m_prev[hd]
        m_row.append(jnp.maximum(log_inter, jnp.max(log_d, axis=1, keepdims=True)))
        d = jnp.exp(log_d - m_row[hd])
        inter.append(jnp.exp(log_inter - m_row[hd]))
        s.append((jnp.dot(q[hd], kt[hd].astype(BF16), preferred_element_type=F32) * d).astype(BF16))

    o_pre = [proj(O_OFF, DV, hd) for hd in heads]
    z_pre = [proj(Z_OFF, DV, hd) for hd in heads]

    intra = [jnp.dot(s[hd], v_ext[hd], preferred_element_type=F32) for hd in heads]
    cross = [jnp.dot(q[hd], c_ext[hd].astype(BF16), preferred_element_type=F32) for hd in heads]

    for hd in heads:
        c_new, m_new = _state_update(c_ext[hd], m_prev[hd], kt[hd], v_ext[hd], li_r[hd], b_r[hd],
                                     b_c[hd][n - 1:n, :])
        c_ref[hd] = c_new
        m_ref[hd] = jnp.broadcast_to(m_new, (SUBLANES, LANES))

    acc = h1
    for hd in heads:
        num = intra[hd][:, :DV] + inter[hd] * cross[hd][:, :DV]
        den = intra[hd][:, DV:DV + 1] + inter[hd] * cross[hd][:, DV:DV + 1]
        hh = num * (1.0 / jnp.maximum(jnp.abs(den), jnp.exp(-m_row[hd])))
        hn = hh * lax.rsqrt(jnp.mean(hh * hh, axis=-1, keepdims=True) + RMS_EPS)
        hn = hn * hw_ref[:, hd * DV:(hd + 1) * DV]
        gated = hn * jax.nn.sigmoid(o_pre[hd]) * (z_pre[hd] * jax.nn.sigmoid(z_pre[hd]))
        acc = acc + jnp.dot(gated.astype(BF16), wo_ref[hd * DV:(hd + 1) * DV, :],
                            preferred_element_type=F32)
    o_ref[0] = _rms_scale(acc, fw_ref[...])


def _mlstm_layer(h1, c0, m0, nw, w, wkt, wg, gb, hw, wo, fw):
    bsz, t_len, d = h1.shape
    n_l = MLSTM_L
    return pl.pallas_call(
        _mlstm_layer_kernel,
        grid=(bsz, t_len // n_l),
        in_specs=[pl.BlockSpec((1, n_l, d), lambda b, t: (b, t, 0))]
        + [_resident(a.shape) for a in (c0, m0, nw, w, wkt, wg, gb, hw, wo, fw)],
        out_specs=pl.BlockSpec((1, n_l, d), lambda b, t: (b, t, 0)),
        out_shape=jax.ShapeDtypeStruct((bsz, t_len, d), F32),
        scratch_shapes=[pltpu.VMEM((N_HEADS, DK, DV_EXT), F32),
                        pltpu.VMEM((N_HEADS, SUBLANES, LANES), F32)],
        compiler_params=pltpu.CompilerParams(
            dimension_semantics=("arbitrary", "arbitrary"), vmem_limit_bytes=VMEM_LIMIT),
        name="mlstm_layer",
    )(h1, c0, m0, nw, w, wkt, wg, gb, hw, wo, fw)


def _meta_state_kernel(h_ref, nw_ref, wv_ref, wkt_ref, wg_ref, gb_ref, c_ref, m_ref):
    h1 = h_ref[...]
    n = h1.shape[0]
    u = _rms_scale(h1, nw_ref[...]).astype(BF16)
    g, g_t, b, b_t = _gate_columns(u, wg_ref, gb_ref)
    ones = _ones_block(n)
    for hd in range(N_HEADS):
        v = jnp.dot(u, wv_ref[:, hd * DV:(hd + 1) * DV], preferred_element_type=F32)
        v_ext = jnp.concatenate([v.astype(BF16), ones], axis=1)
        b_c = b[:, N_HEADS + hd:N_HEADS + hd + 1]
        c_new, m_new = _state_update(
            jnp.zeros((DK, DV_EXT), F32), jnp.zeros((1, 1), F32), _k_transposed(wkt_ref, u, hd), v_ext,
            g_t[hd:hd + 1, :], b_t[N_HEADS + hd:N_HEADS + hd + 1, :], b_c[n - 1:n, :])
        c_ref[hd] = c_new
        m_ref[hd] = jnp.broadcast_to(m_new, (SUBLANES, LANES))


def _meta_state(h1_meta, nw, w, wkt, wg, gb):
    full = lambda a: pl.BlockSpec(a.shape, lambda i: (0,) * a.ndim)
    return pl.pallas_call(
        _meta_state_kernel,
        grid=(1,),
        in_specs=[full(h1_meta), full(nw),
                  pl.BlockSpec((D_MODEL, E_MLSTM), lambda i: (0, V_OFF // E_MLSTM)),
                  full(wkt), full(wg), full(gb)],
        out_specs=[pl.BlockSpec((N_HEADS, DK, DV_EXT), lambda i: (0, 0, 0)),
                   pl.BlockSpec((N_HEADS, SUBLANES, LANES), lambda i: (0, 0, 0))],
        out_shape=[jax.ShapeDtypeStruct((N_HEADS, DK, DV_EXT), F32),
                   jax.ShapeDtypeStruct((N_HEADS, SUBLANES, LANES), F32)],
        compiler_params=pltpu.CompilerParams(
            dimension_semantics=("arbitrary",), vmem_limit_bytes=VMEM_LIMIT),
        name="mlstm_meta_state",
    )(h1_meta, nw, w, wkt, wg, gb)


def kernel(x, meta_tokens, norm_w, conv_in_w, conv_w, conv_out_w, mlstm_in_w, mlstm_gate_b,
           mlstm_head_norm_w, mlstm_out_w, final_norm_w):
    w_in = conv_in_w[0].astype(BF16)
    cw = conv_w[0]
    w_out = conv_out_w[0].astype(BF16)
    mw = mlstm_in_w[0]
    w = mw[:, :MLSTM_PROJ_W].astype(BF16)
    wkt = mw[:, QK:2 * QK].reshape(D_MODEL, N_HEADS, DK).transpose(1, 2, 0).astype(BF16)
    reps = 3
    wg = jnp.pad(jnp.tile(mw[:, MLSTM_PROJ_W:], (1, reps)), ((0, 0), (0, LANES - reps * N_GATES))).astype(BF16)
    gb = jnp.pad(jnp.tile(mlstm_gate_b[0], reps), (0, LANES - reps * N_GATES)).reshape(1, LANES)
    hw = mlstm_head_norm_w[0].reshape(1, E_MLSTM)
    wo = mlstm_out_w[0].astype(BF16)
    nw0 = norm_w[0].reshape(1, D_MODEL)
    nw1 = norm_w[1].reshape(1, D_MODEL)
    fw = final_norm_w.reshape(1, D_MODEL)

    zero_halo = jnp.zeros((SUBLANES, E_CONV), F32)
    h1_meta, meta_tail = _conv_layer(meta_tokens[None], zero_halo, nw0, w_in, cw, w_out,
                                     tm=N_META, emit_tail=True)
    c0, m0 = _meta_state(h1_meta[0], nw1, w, wkt, wg, gb)

    (h1,) = _conv_layer(x, meta_tail, nw0, w_in, cw, w_out, tm=CONV_TM, emit_tail=False)
    return _mlstm_layer(h1, c0, m0, nw1, w, wkt, wg, gb, hw, wo, fw)
```

```python
import functools

import jax
import jax.numpy as jnp
from jax import lax
from jax.experimental import pallas as pl
from jax.experimental.pallas import tpu as pltpu

D_MODEL = 1024
N_META = 16
E_CONV = 2048
CONV_WIDTH = 3
E_MLSTM = 2048
N_HEADS = 4
DV = E_MLSTM // N_HEADS
DK = DV // 2
QK = N_HEADS * DK
N_GATES = 2 * N_HEADS
RMS_EPS = 1e-6

LANES = 128
SUBLANES = 8
DV_EXT = DV + LANES
V_OFF = 2 * QK
O_OFF = V_OFF + E_MLSTM
Z_OFF = O_OFF + E_MLSTM
MLSTM_PROJ_W = Z_OFF + E_MLSTM

CONV_TM = 512
CONV_TE = 512
MLSTM_L = 256
VMEM_LIMIT = 56 * 1024 * 1024

F32 = jnp.float32
BF16 = jnp.bfloat16


def _rms_scale(x, w_row):
    ms = jnp.mean(x * x, axis=-1, keepdims=True)
    return x * lax.rsqrt(ms + RMS_EPS) * w_row


def _log_sigmoid(x):
    return jnp.minimum(x, 0.0) - jnp.log1p(jnp.exp(-jnp.abs(x)))


def _resident(shape):
    nd = len(shape)
    return pl.BlockSpec(shape, lambda *_: (0,) * nd, pipeline_mode=pl.Buffered(1))


def _conv_layer_kernel(x_ref, halo_in_ref, nw_ref, w_in_ref, cw_ref, w_out_ref, *rest,
                       te, emit_tail):
    if emit_tail:
        h_ref, tail_ref, halo_ref = rest
    else:
        h_ref, halo_ref = rest

    @pl.when(pl.program_id(1) == 0)
    def _():
        halo_ref[...] = halo_in_ref[...]

    x = x_ref[0]
    tm = x.shape[0]
    u = _rms_scale(x, nw_ref[...]).astype(BF16)
    row = lax.broadcasted_iota(jnp.int32, (tm, te), 0)
    n_chunks = E_CONV // te

    def in_proj(e):
        return [jnp.dot(u, w_in_ref[:, k * E_CONV + e * te:k * E_CONV + (e + 1) * te],
                        preferred_element_type=F32) for k in range(4)]

    acc = x
    p_next = in_proj(0)
    for e in range(n_chunks):
        lo = e * te
        b_gate, c_gate, xin, z = p_next
        if e + 1 < n_chunks:
            p_next = in_proj(e + 1)
        cx = c_gate * xin
        halo = halo_ref[:, lo:lo + te]
        prev1 = jnp.where(row == 0, halo[7:8, :], pltpu.roll(cx, 1, 0))
        prev2 = jnp.where(row == 0, halo[6:7, :],
                          jnp.where(row == 1, halo[7:8, :], pltpu.roll(cx, 2, 0)))
        cw = cw_ref[:, lo:lo + te]
        y = cw[0:1, :] * prev2 + cw[1:2, :] * prev1 + cw[2:3, :] * cx
        halo_ref[:, lo:lo + te] = cx[tm - SUBLANES:, :]
        g = (z * jax.nn.sigmoid(z)) * b_gate * y
        acc = acc + jnp.dot(g.astype(BF16), w_out_ref[lo:lo + te, :], preferred_element_type=F32)
    h_ref[0] = acc
    if emit_tail:
        tail_ref[...] = halo_ref[...]


def _conv_layer(x, halo_in, nw, w_in, cw, w_out, *, tm, emit_tail):
    bsz, t_len, d = x.shape
    out_shape = [jax.ShapeDtypeStruct((bsz, t_len, d), F32)]
    out_specs = [pl.BlockSpec((1, tm, d), lambda b, t: (b, t, 0))]
    if emit_tail:
        out_shape.append(jax.ShapeDtypeStruct((SUBLANES, E_CONV), F32))
        out_specs.append(pl.BlockSpec((SUBLANES, E_CONV), lambda b, t: (0, 0)))
    return pl.pallas_call(
        functools.partial(_conv_layer_kernel, te=min(CONV_TE, E_CONV), emit_tail=emit_tail),
        grid=(bsz, t_len // tm),
        in_specs=[
            pl.BlockSpec((1, tm, d), lambda b, t: (b, t, 0)),
            _resident(halo_in.shape),
            _resident(nw.shape),
            _resident(w_in.shape),
            _resident(cw.shape),
            _resident(w_out.shape),
        ],
        out_specs=out_specs,
        out_shape=out_shape,
        scratch_shapes=[pltpu.VMEM((SUBLANES, E_CONV), F32)],
        compiler_params=pltpu.CompilerParams(
            dimension_semantics=("arbitrary", "arbitrary"), vmem_limit_bytes=VMEM_LIMIT),
        name="conv_layer_meta" if emit_tail else "conv_layer",
    )(x, halo_in, nw, w_in, cw, w_out)


def _gate_columns(u, wg_ref, gb_ref):
    n = u.shape[0]
    g = jnp.dot(u, wg_ref[...], preferred_element_type=F32) + gb_ref[...]
    lf = _log_sigmoid(g)
    hi = lf.astype(BF16).astype(F32)
    mid = (lf - hi).astype(BF16).astype(F32)
    low = lf - hi - mid
    lane = lax.broadcasted_iota(jnp.int32, (n, LANES), 1)
    terms = jnp.where(lane < N_GATES, hi, jnp.where(lane < 2 * N_GATES, mid, low)).astype(BF16)
    r = lax.broadcasted_iota(jnp.int32, (n, n), 0)
    c = lax.broadcasted_iota(jnp.int32, (n, n), 1)
    tril = (r >= c).astype(BF16)
    part = jnp.dot(tril, terms, preferred_element_type=F32)
    b = part + pltpu.roll(part, LANES - N_GATES, 1) + pltpu.roll(part, LANES - 2 * N_GATES, 1)
    return g, g.T, b, b.T


def _state_update(c_ext, m_prev, kt, v, li_r, b_r, b_last):
    logw = b_last - b_r + li_r
    m_new = jnp.maximum(b_last + m_prev, jnp.max(logw, axis=1, keepdims=True))
    decay = jnp.exp(b_last + m_prev - m_new)
    kw = kt * jnp.exp(logw - m_new)
    c_new = decay * c_ext[:, :DV] + jnp.dot(kw.astype(BF16), v, preferred_element_type=F32)
    n_new = decay * c_ext[:, DV:DV + 1] + jnp.sum(kw, axis=1, keepdims=True)
    lane = lax.broadcasted_iota(jnp.int32, (DK, LANES), 1)
    return jnp.concatenate([c_new, jnp.where(lane == 0, n_new, 0.0)], axis=1), m_new


def _k_transposed(wkt_ref, u, hd):
    return lax.dot_general(wkt_ref[hd], u, (((1,), (1,)), ((), ())), preferred_element_type=F32)


def _mlstm_layer_kernel(h_ref, c0_ref, m0_ref, nw_ref, w_ref, wkt_ref, wg_ref, gb_ref, hw_ref,
                        wo_ref, fw_ref, o_ref, c_ref, m_ref):
    @pl.when(pl.program_id(1) == 0)
    def _():
        c_ref[...] = c0_ref[...]
        m_ref[...] = m0_ref[...]

    h1 = h_ref[0]
    n = h1.shape[0]
    u = _rms_scale(h1, nw_ref[...]).astype(BF16)
    g, g_t, b, b_t = _gate_columns(u, wg_ref, gb_ref)
    r = lax.broadcasted_iota(jnp.int32, (n, n), 0)
    c = lax.broadcasted_iota(jnp.int32, (n, n), 1)
    causal = r >= c
    heads = range(N_HEADS)

    def proj(off, width, hd):
        return jnp.dot(u, w_ref[:, off + hd * width:off + (hd + 1) * width], preferred_element_type=F32)

    def qk_proj(hd):
        q = (proj(0, DK, hd) * (DK ** -0.5)).astype(BF16)
        return q, _k_transposed(wkt_ref, u, hd)

    def v_proj(hd):
        return proj(V_OFF, DV, hd).astype(BF16)

    li_r = [g_t[hd:hd + 1, :] for hd in heads]
    b_r = [b_t[N_HEADS + hd:N_HEADS + hd + 1, :] for hd in heads]
    b_c = [b[:, N_HEADS + hd:N_HEADS + hd + 1] for hd in heads]
    m_prev = [m_ref[hd][0:1, 0:1] for hd in heads]
    m_row, inter, d = [], [], []
    for hd in heads:
        log_d = jnp.where(causal, b_c[hd] - b_r[hd] + li_r[hd], -jnp.inf)
        log_inter = b_c[hd] + m_prev[hd]
        m_row.append(jnp.maximum(log_inter, jnp.max(log_d, axis=1, keepdims=True)))
        d.append(jnp.exp(log_d - m_row[hd]))
        inter.append(jnp.exp(log_inter - m_row[hd]))

    (q, kt), v = qk_proj(0), v_proj(0)
    acc = h1
    for hd in heads:
        last = hd + 1 == N_HEADS
        c_ext = c_ref[hd]
        sd = jnp.dot(q, kt.astype(BF16), preferred_element_type=F32)
        o_gate = jax.nn.sigmoid(proj(O_OFF, DV, hd)) * hw_ref[:, hd * DV:(hd + 1) * DV]
        sd = sd * d[hd]
        intra = jnp.dot(sd.astype(BF16), v, preferred_element_type=F32)
        cross = jnp.dot(q, c_ext.astype(BF16), preferred_element_type=F32)
        z_pre = proj(Z_OFF, DV, hd)
        num = intra + inter[hd] * cross[:, :DV]
        den = jnp.sum(sd, axis=1, keepdims=True) + inter[hd] * cross[:, DV:DV + 1]
        hh = num * (1.0 / jnp.maximum(jnp.abs(den), jnp.exp(-m_row[hd])))
        hn = hh * lax.rsqrt(jnp.mean(hh * hh, axis=-1, keepdims=True) + RMS_EPS)
        if not last:
            q_next, kt_next = qk_proj(hd + 1)
        c_new, m_new = _state_update(c_ext, m_prev[hd], kt, v, li_r[hd], b_r[hd], b_c[hd][n - 1:n, :])
        c_ref[hd] = c_new
        m_ref[hd] = jnp.broadcast_to(m_new, (SUBLANES, LANES))
        if not last:
            v_next = v_proj(hd + 1)
        gated = hn * o_gate * (z_pre * jax.nn.sigmoid(z_pre))
        acc = acc + jnp.dot(gated.astype(BF16), wo_ref[hd * DV:(hd + 1) * DV, :],
                            preferred_element_type=F32)
        if not last:
            q, kt, v = q_next, kt_next, v_next
    o_ref[0] = _rms_scale(acc, fw_ref[...])


def _mlstm_layer(h1, c0, m0, nw, w, wkt, wg, gb, hw, wo, fw):
    bsz, t_len, d = h1.shape
    n_l = MLSTM_L
    return pl.pallas_call(
        _mlstm_layer_kernel,
        grid=(bsz, t_len // n_l),
        in_specs=[pl.BlockSpec((1, n_l, d), lambda b, t: (b, t, 0))]
        + [_resident(a.shape) for a in (c0, m0, nw, w, wkt, wg, gb, hw, wo, fw)],
        out_specs=pl.BlockSpec((1, n_l, d), lambda b, t: (b, t, 0)),
        out_shape=jax.ShapeDtypeStruct((bsz, t_len, d), F32),
        scratch_shapes=[pltpu.VMEM((N_HEADS, DK, DV_EXT), F32),
                        pltpu.VMEM((N_HEADS, SUBLANES, LANES), F32)],
        compiler_params=pltpu.CompilerParams(
            dimension_semantics=("arbitrary", "arbitrary"), vmem_limit_bytes=VMEM_LIMIT),
        name="mlstm_layer",
    )(h1, c0, m0, nw, w, wkt, wg, gb, hw, wo, fw)


def _meta_state_kernel(h_ref, nw_ref, wv_ref, wkt_ref, wg_ref, gb_ref, c_ref, m_ref):
    h1 = h_ref[...]
    n = h1.shape[0]
    u = _rms_scale(h1, nw_ref[...]).astype(BF16)
    g, g_t, b, b_t = _gate_columns(u, wg_ref, gb_ref)
    for hd in range(N_HEADS):
        v = jnp.dot(u, wv_ref[:, hd * DV:(hd + 1) * DV], preferred_element_type=F32).astype(BF16)
        b_c = b[:, N_HEADS + hd:N_HEADS + hd + 1]
        c_new, m_new = _state_update(
            jnp.zeros((DK, DV_EXT), F32), jnp.zeros((1, 1), F32), _k_transposed(wkt_ref, u, hd), v,
            g_t[hd:hd + 1, :], b_t[N_HEADS + hd:N_HEADS + hd + 1, :], b_c[n - 1:n, :])
        c_ref[hd] = c_new
        m_ref[hd] = jnp.broadcast_to(m_new, (SUBLANES, LANES))


def _meta_state(h1_meta, nw, w, wkt, wg, gb):
    full = lambda a: pl.BlockSpec(a.shape, lambda i: (0,) * a.ndim)
    return pl.pallas_call(
        _meta_state_kernel,
        grid=(1,),
        in_specs=[full(h1_meta), full(nw),
                  pl.BlockSpec((D_MODEL, E_MLSTM), lambda i: (0, V_OFF // E_MLSTM)),
                  full(wkt), full(wg), full(gb)],
        out_specs=[pl.BlockSpec((N_HEADS, DK, DV_EXT), lambda i: (0, 0, 0)),
                   pl.BlockSpec((N_HEADS, SUBLANES, LANES), lambda i: (0, 0, 0))],
        out_shape=[jax.ShapeDtypeStruct((N_HEADS, DK, DV_EXT), F32),
                   jax.ShapeDtypeStruct((N_HEADS, SUBLANES, LANES), F32)],
        compiler_params=pltpu.CompilerParams(
            dimension_semantics=("arbitrary",), vmem_limit_bytes=VMEM_LIMIT),
        name="mlstm_meta_state",
    )(h1_meta, nw, w, wkt, wg, gb)


def kernel(x, meta_tokens, norm_w, conv_in_w, conv_w, conv_out_w, mlstm_in_w, mlstm_gate_b,
           mlstm_head_norm_w, mlstm_out_w, final_norm_w):
    w_in = conv_in_w[0].astype(BF16)
    cw = conv_w[0]
    w_out = conv_out_w[0].astype(BF16)
    mw = mlstm_in_w[0]
    w = mw.astype(BF16)
    wkt = mw[:, QK:2 * QK].reshape(D_MODEL, N_HEADS, DK).transpose(1, 2, 0).astype(BF16)
    reps = 3
    wg = jnp.pad(jnp.tile(mw[:, MLSTM_PROJ_W:], (1, reps)), ((0, 0), (0, LANES - reps * N_GATES))).astype(BF16)
    gb = jnp.pad(jnp.tile(mlstm_gate_b[0], reps), (0, LANES - reps * N_GATES)).reshape(1, LANES)
    hw = mlstm_head_norm_w[0].reshape(1, E_MLSTM)
    wo = mlstm_out_w[0].astype(BF16)
    nw0 = norm_w[0].reshape(1, D_MODEL)
    nw1 = norm_w[1].reshape(1, D_MODEL)
    fw = final_norm_w.reshape(1, D_MODEL)

    zero_halo = jnp.zeros((SUBLANES, E_CONV), F32)
    h1_meta, meta_tail = _conv_layer(meta_tokens[None], zero_halo, nw0, w_in, cw, w_out,
                                     tm=N_META, emit_tail=True)
    c0, m0 = _meta_state(h1_meta[0], nw1, w, wkt, wg, gb)

    (h1,) = _conv_layer(x, meta_tail, nw0, w_in, cw, w_out, tm=CONV_TM, emit_tail=False)
    return _mlstm_layer(h1, c0, m0, nw1, w, wkt, wg, gb, hw, wo, fw)
```

```python
import functools

import jax
import jax.numpy as jnp
from jax import lax
from jax.experimental import pallas as pl
from jax.experimental.pallas import tpu as pltpu

D_MODEL = 1024
N_META = 16
E_CONV = 2048
CONV_WIDTH = 3
E_MLSTM = 2048
N_HEADS = 4
DV = E_MLSTM // N_HEADS
DK = DV // 2
QK = N_HEADS * DK
N_GATES = 2 * N_HEADS
RMS_EPS = 1e-6

LANES = 128
SUBLANES = 8
DV_EXT = DV + LANES
V_OFF = 2 * QK
O_OFF = V_OFF + E_MLSTM
Z_OFF = O_OFF + E_MLSTM
MLSTM_PROJ_W = Z_OFF + E_MLSTM

CONV_TM = 1024
CONV_TE = 1024
MLSTM_L = 256
MLSTM_TM = 256
VMEM_LIMIT = 56 * 1024 * 1024

F32 = jnp.float32
BF16 = jnp.bfloat16


def _rms_scale(x, w_row):
    ms = jnp.mean(x * x, axis=-1, keepdims=True)
    return x * lax.rsqrt(ms + RMS_EPS) * w_row


def _log_sigmoid(x):
    return jnp.minimum(x, 0.0) - jnp.log1p(jnp.exp(-jnp.abs(x)))


def _resident(shape):
    nd = len(shape)
    return pl.BlockSpec(shape, lambda *_: (0,) * nd, pipeline_mode=pl.Buffered(1))


def _conv_layer_kernel(x_ref, halo_in_ref, nw_ref, w_in_ref, cw_ref, w_out_ref, *rest,
                       te, emit_tail):
    if emit_tail:
        h_ref, tail_ref, halo_ref = rest
    else:
        h_ref, halo_ref = rest

    @pl.when(pl.program_id(1) == 0)
    def _():
        halo_ref[...] = halo_in_ref[...]

    x = x_ref[0]
    tm = x.shape[0]
    u = _rms_scale(x, nw_ref[...]).astype(BF16)
    row = lax.broadcasted_iota(jnp.int32, (tm, te), 0)
    n_chunks = E_CONV // te

    def in_proj(e):
        return [jnp.dot(u, w_in_ref[:, k * E_CONV + e * te:k * E_CONV + (e + 1) * te],
                        preferred_element_type=F32) for k in range(4)]

    acc = x
    p_next = in_proj(0)
    for e in range(n_chunks):
        lo = e * te
        b_gate, c_gate, xin, z = p_next
        if e + 1 < n_chunks:
            p_next = in_proj(e + 1)
        cx = c_gate * xin
        halo = halo_ref[:, lo:lo + te]
        prev1 = jnp.where(row == 0, halo[7:8, :], pltpu.roll(cx, 1, 0))
        prev2 = jnp.where(row == 0, halo[6:7, :],
                          jnp.where(row == 1, halo[7:8, :], pltpu.roll(cx, 2, 0)))
        cw = cw_ref[:, lo:lo + te]
        y = cw[0:1, :] * prev2 + cw[1:2, :] * prev1 + cw[2:3, :] * cx
        halo_ref[:, lo:lo + te] = cx[tm - SUBLANES:, :]
        g = (z * jax.nn.sigmoid(z)) * b_gate * y
        acc = acc + jnp.dot(g.astype(BF16), w_out_ref[lo:lo + te, :], preferred_element_type=F32)
    h_ref[0] = acc
    if emit_tail:
        tail_ref[...] = halo_ref[...]


def _conv_layer(x, halo_in, nw, w_in, cw, w_out, *, tm, emit_tail):
    bsz, t_len, d = x.shape
    out_shape = [jax.ShapeDtypeStruct((bsz, t_len, d), F32)]
    out_specs = [pl.BlockSpec((1, tm, d), lambda b, t: (b, t, 0))]
    if emit_tail:
        out_shape.append(jax.ShapeDtypeStruct((SUBLANES, E_CONV), F32))
        out_specs.append(pl.BlockSpec((SUBLANES, E_CONV), lambda b, t: (0, 0)))
    return pl.pallas_call(
        functools.partial(_conv_layer_kernel, te=min(CONV_TE, E_CONV), emit_tail=emit_tail),
        grid=(bsz, t_len // tm),
        in_specs=[
            pl.BlockSpec((1, tm, d), lambda b, t: (b, t, 0)),
            _resident(halo_in.shape),
            _resident(nw.shape),
            _resident(w_in.shape),
            _resident(cw.shape),
            _resident(w_out.shape),
        ],
        out_specs=out_specs,
        out_shape=out_shape,
        scratch_shapes=[pltpu.VMEM((SUBLANES, E_CONV), F32)],
        compiler_params=pltpu.CompilerParams(
            dimension_semantics=("arbitrary", "arbitrary"), vmem_limit_bytes=VMEM_LIMIT),
        name="conv_layer_meta" if emit_tail else "conv_layer",
    )(x, halo_in, nw, w_in, cw, w_out)


def _gate_preacts(u, wg_ref, gb_ref):
    return jnp.dot(u, wg_ref[...], preferred_element_type=F32) + gb_ref[...]


def _gate_columns(g):
    n = g.shape[0]
    lf = _log_sigmoid(g)
    hi = lf.astype(BF16).astype(F32)
    mid = (lf - hi).astype(BF16).astype(F32)
    low = lf - hi - mid
    lane = lax.broadcasted_iota(jnp.int32, (n, LANES), 1)
    terms = jnp.where(lane < N_GATES, hi, jnp.where(lane < 2 * N_GATES, mid, low)).astype(BF16)
    r = lax.broadcasted_iota(jnp.int32, (n, n), 0)
    c = lax.broadcasted_iota(jnp.int32, (n, n), 1)
    tril = (r >= c).astype(BF16)
    part = jnp.dot(tril, terms, preferred_element_type=F32)
    b = part + pltpu.roll(part, LANES - N_GATES, 1) + pltpu.roll(part, LANES - 2 * N_GATES, 1)
    return g.T, b, b.T


def _state_weights(li_r, b_r, b_last, m_prev):
    logw = b_last - b_r + li_r
    m_new = jnp.maximum(b_last + m_prev, jnp.max(logw, axis=1, keepdims=True))
    return m_new, jnp.exp(b_last + m_prev - m_new), jnp.exp(logw - m_new)


def _state_apply(c_ext, decay, w_r, kt, v):
    kw = kt * w_r
    c_new = decay * c_ext[:, :DV] + jnp.dot(kw.astype(BF16), v, preferred_element_type=F32)
    n_new = decay * c_ext[:, DV:DV + 1] + jnp.sum(kw, axis=1, keepdims=True)
    lane = lax.broadcasted_iota(jnp.int32, (DK, LANES), 1)
    return jnp.concatenate([c_new, jnp.where(lane == 0, n_new, 0.0)], axis=1)


def _k_transposed(wkt_ref, u, hd):
    return lax.dot_general(wkt_ref[hd], u, (((1,), (1,)), ((), ())), preferred_element_type=F32)


def _mlstm_layer_kernel(h_ref, c0_ref, m0_ref, nw_ref, w_ref, wkt_ref, wg_ref, gb_ref, hw_ref,
                        wo_ref, fw_ref, o_ref, c_ref, m_ref):
    @pl.when(pl.program_id(1) == 0)
    def _():
        c_ref[...] = c0_ref[...]
        m_ref[...] = m0_ref[...]

    n = MLSTM_L
    n_sub = h_ref.shape[1] // n
    u_all = _rms_scale(h_ref[0], nw_ref[...]).astype(BF16)
    g_all = _gate_preacts(u_all, wg_ref, gb_ref)
    r = lax.broadcasted_iota(jnp.int32, (n, n), 0)
    c = lax.broadcasted_iota(jnp.int32, (n, n), 1)
    causal = r >= c
    heads = range(N_HEADS)

    m_run = [m_ref[hd][0:1, 0:1] for hd in heads]
    gate_terms = []
    for ci in range(n_sub):
        g_t, b, b_t = _gate_columns(g_all[ci * n:(ci + 1) * n, :])
        for hd in heads:
            li_r = g_t[hd:hd + 1, :]
            b_r = b_t[N_HEADS + hd:N_HEADS + hd + 1, :]
            b_c = b[:, N_HEADS + hd:N_HEADS + hd + 1]
            log_d = jnp.where(causal, b_c - b_r + li_r, -jnp.inf)
            log_inter = b_c + m_run[hd]
            m_row = jnp.maximum(log_inter, jnp.max(log_d, axis=1, keepdims=True))
            m_new, decay, w_r = _state_weights(li_r, b_r, b_c[n - 1:n, :], m_run[hd])
            gate_terms.append((jnp.exp(log_d - m_row), jnp.exp(log_inter - m_row), jnp.exp(-m_row),
                               decay, w_r))
            m_run[hd] = m_new
    for hd in heads:
        m_ref[hd] = jnp.broadcast_to(m_run[hd], (SUBLANES, LANES))

    def proj(u, off, width, hd):
        return jnp.dot(u, w_ref[:, off + hd * width:off + (hd + 1) * width], preferred_element_type=F32)

    def qkv_proj(item):
        ci, hd = divmod(item, N_HEADS)
        u = u_all[ci * n:(ci + 1) * n, :]
        q = (proj(u, 0, DK, hd) * (DK ** -0.5)).astype(BF16)
        return q, _k_transposed(wkt_ref, u, hd), proj(u, V_OFF, DV, hd).astype(BF16)

    n_items = n_sub * N_HEADS
    q, kt, v = qkv_proj(0)
    for item in range(n_items):
        ci, hd = divmod(item, N_HEADS)
        u = u_all[ci * n:(ci + 1) * n, :]
        d, inter, inv_floor, decay, w_r = gate_terms[item]
        if hd == 0:
            acc = h_ref[0, ci * n:(ci + 1) * n, :]
        c_ext = c_ref[hd]
        sd = jnp.dot(q, kt.astype(BF16), preferred_element_type=F32)
        o_gate = jax.nn.sigmoid(proj(u, O_OFF, DV, hd)) * hw_ref[:, hd * DV:(hd + 1) * DV]
        sd = sd * d
        intra = jnp.dot(sd.astype(BF16), v, preferred_element_type=F32)
        cross = jnp.dot(q, c_ext.astype(BF16), preferred_element_type=F32)
        z_pre = proj(u, Z_OFF, DV, hd)
        num = intra + inter * cross[:, :DV]
        den = jnp.sum(sd, axis=1, keepdims=True) + inter * cross[:, DV:DV + 1]
        hh = num * (1.0 / jnp.maximum(jnp.abs(den), inv_floor))
        hn = hh * lax.rsqrt(jnp.mean(hh * hh, axis=-1, keepdims=True) + RMS_EPS)
        c_ref[hd] = _state_apply(c_ext, decay, w_r, kt, v)
        if item + 1 < n_items:
            q, kt, v = qkv_proj(item + 1)
        gated = hn * o_gate * (z_pre * jax.nn.sigmoid(z_pre))
        acc = acc + jnp.dot(gated.astype(BF16), wo_ref[hd * DV:(hd + 1) * DV, :],
                            preferred_element_type=F32)
        if hd + 1 == N_HEADS:
            o_ref[0, ci * n:(ci + 1) * n, :] = _rms_scale(acc, fw_ref[...])


def _mlstm_layer(h1, c0, m0, nw, w, wkt, wg, gb, hw, wo, fw):
    bsz, t_len, d = h1.shape
    n_l = MLSTM_TM
    return pl.pallas_call(
        _mlstm_layer_kernel,
        grid=(bsz, t_len // n_l),
        in_specs=[pl.BlockSpec((1, n_l, d), lambda b, t: (b, t, 0))]
        + [_resident(a.shape) for a in (c0, m0, nw, w, wkt, wg, gb, hw, wo, fw)],
        out_specs=pl.BlockSpec((1, n_l, d), lambda b, t: (b, t, 0)),
        out_shape=jax.ShapeDtypeStruct((bsz, t_len, d), F32),
        scratch_shapes=[pltpu.VMEM((N_HEADS, DK, DV_EXT), F32),
                        pltpu.VMEM((N_HEADS, SUBLANES, LANES), F32)],
        compiler_params=pltpu.CompilerParams(
            dimension_semantics=("arbitrary", "arbitrary"), vmem_limit_bytes=VMEM_LIMIT),
        name="mlstm_layer",
    )(h1, c0, m0, nw, w, wkt, wg, gb, hw, wo, fw)


def _meta_state_kernel(h_ref, nw_ref, wv_ref, wkt_ref, wg_ref, gb_ref, c_ref, m_ref):
    h1 = h_ref[...]
    n = h1.shape[0]
    u = _rms_scale(h1, nw_ref[...]).astype(BF16)
    g_t, b, b_t = _gate_columns(_gate_preacts(u, wg_ref, gb_ref))
    for hd in range(N_HEADS):
        v = jnp.dot(u, wv_ref[:, hd * DV:(hd + 1) * DV], preferred_element_type=F32).astype(BF16)
        b_last = b[n - 1:n, N_HEADS + hd:N_HEADS + hd + 1]
        m_new, decay, w_r = _state_weights(g_t[hd:hd + 1, :], b_t[N_HEADS + hd:N_HEADS + hd + 1, :],
                                           b_last, jnp.zeros((1, 1), F32))
        c_ref[hd] = _state_apply(jnp.zeros((DK, DV_EXT), F32), decay, w_r, _k_transposed(wkt_ref, u, hd), v)
        m_ref[hd] = jnp.broadcast_to(m_new, (SUBLANES, LANES))


def _meta_state(h1_meta, nw, w, wkt, wg, gb):
    full = lambda a: pl.BlockSpec(a.shape, lambda i: (0,) * a.ndim)
    return pl.pallas_call(
        _meta_state_kernel,
        grid=(1,),
        in_specs=[full(h1_meta), full(nw),
                  pl.BlockSpec((D_MODEL, E_MLSTM), lambda i: (0, V_OFF // E_MLSTM)),
                  full(wkt), full(wg), full(gb)],
        out_specs=[pl.BlockSpec((N_HEADS, DK, DV_EXT), lambda i: (0, 0, 0)),
                   pl.BlockSpec((N_HEADS, SUBLANES, LANES), lambda i: (0, 0, 0))],
        out_shape=[jax.ShapeDtypeStruct((N_HEADS, DK, DV_EXT), F32),
                   jax.ShapeDtypeStruct((N_HEADS, SUBLANES, LANES), F32)],
        compiler_params=pltpu.CompilerParams(
            dimension_semantics=("arbitrary",), vmem_limit_bytes=VMEM_LIMIT),
        name="mlstm_meta_state",
    )(h1_meta, nw, w, wkt, wg, gb)


def kernel(x, meta_tokens, norm_w, conv_in_w, conv_w, conv_out_w, mlstm_in_w, mlstm_gate_b,
           mlstm_head_norm_w, mlstm_out_w, final_norm_w):
    w_in = conv_in_w[0].astype(BF16)
    cw = conv_w[0]
    w_out = conv_out_w[0].astype(BF16)
    mw = mlstm_in_w[0]
    w = mw.astype(BF16)
    wkt = mw[:, QK:2 * QK].reshape(D_MODEL, N_HEADS, DK).transpose(1, 2, 0).astype(BF16)
    reps = 3
    wg = jnp.pad(jnp.tile(mw[:, MLSTM_PROJ_W:], (1, reps)), ((0, 0), (0, LANES - reps * N_GATES))).astype(BF16)
    gb = jnp.pad(jnp.tile(mlstm_gate_b[0], reps), (0, LANES - reps * N_GATES)).reshape(1, LANES)
    hw = mlstm_head_norm_w[0].reshape(1, E_MLSTM)
    wo = mlstm_out_w[0].astype(BF16)
    nw0 = norm_w[0].reshape(1, D_MODEL)
    nw1 = norm_w[1].reshape(1, D_MODEL)
    fw = final_norm_w.reshape(1, D_MODEL)

    zero_halo = jnp.zeros((SUBLANES, E_CONV), F32)
    h1_meta, meta_tail = _conv_layer(meta_tokens[None], zero_halo, nw0, w_in, cw, w_out,
                                     tm=N_META, emit_tail=True)
    c0, m0 = _meta_state(h1_meta[0], nw1, w, wkt, wg, gb)

    (h1,) = _conv_layer(x, meta_tail, nw0, w_in, cw, w_out, tm=CONV_TM, emit_tail=False)
    return _mlstm_layer(h1, c0, m0, nw1, w, wkt, wg, gb, hw, wo, fw)
```

```python
import functools

import jax
import jax.numpy as jnp
from jax import lax
from jax.experimental import pallas as pl
from jax.experimental.pallas import tpu as pltpu

D_MODEL = 1024
N_META = 16
E_CONV = 2048
CONV_WIDTH = 3
E_MLSTM = 2048
N_HEADS = 4
DV = E_MLSTM // N_HEADS
DK = DV // 2
QK = N_HEADS * DK
N_GATES = 2 * N_HEADS
RMS_EPS = 1e-6

LANES = 128
SUBLANES = 8
DV_EXT = DV + LANES
V_OFF = 2 * QK
O_OFF = V_OFF + E_MLSTM
Z_OFF = O_OFF + E_MLSTM
MLSTM_PROJ_W = Z_OFF + E_MLSTM

CONV_TM = 1024
CONV_TE = 1024
MLSTM_L = 256
MLSTM_TM = 512
VMEM_LIMIT = 56 * 1024 * 1024

F32 = jnp.float32
BF16 = jnp.bfloat16


def _rms_scale(x, w_row):
    ms = jnp.mean(x * x, axis=-1, keepdims=True)
    return x * lax.rsqrt(ms + RMS_EPS) * w_row


def _log_sigmoid(x):
    return jnp.minimum(x, 0.0) - jnp.log1p(jnp.exp(-jnp.abs(x)))


def _resident(shape):
    nd = len(shape)
    return pl.BlockSpec(shape, lambda *_: (0,) * nd, pipeline_mode=pl.Buffered(1))


def _conv_layer_kernel(x_ref, halo_in_ref, nw_ref, w_in_ref, cw_ref, w_out_ref, *rest,
                       te, emit_tail):
    if emit_tail:
        h_ref, tail_ref, halo_ref = rest
    else:
        h_ref, halo_ref = rest

    @pl.when(pl.program_id(1) == 0)
    def _():
        halo_ref[...] = halo_in_ref[...]

    x = x_ref[0]
    tm = x.shape[0]
    u = _rms_scale(x, nw_ref[...]).astype(BF16)
    row = lax.broadcasted_iota(jnp.int32, (tm, te), 0)
    n_chunks = E_CONV // te

    def in_proj(e):
        return [jnp.dot(u, w_in_ref[:, k * E_CONV + e * te:k * E_CONV + (e + 1) * te],
                        preferred_element_type=F32) for k in range(4)]

    acc = x
    p_next = in_proj(0)
    for e in range(n_chunks):
        lo = e * te
        b_gate, c_gate, xin, z = p_next
        if e + 1 < n_chunks:
            p_next = in_proj(e + 1)
        cx = c_gate * xin
        halo = halo_ref[:, lo:lo + te]
        prev1 = jnp.where(row == 0, halo[7:8, :], pltpu.roll(cx, 1, 0))
        prev2 = jnp.where(row == 0, halo[6:7, :],
                          jnp.where(row == 1, halo[7:8, :], pltpu.roll(cx, 2, 0)))
        cw = cw_ref[:, lo:lo + te]
        y = cw[0:1, :] * prev2 + cw[1:2, :] * prev1 + cw[2:3, :] * cx
        halo_ref[:, lo:lo + te] = cx[tm - SUBLANES:, :]
        g = (z * jax.nn.sigmoid(z)) * b_gate * y
        acc = acc + jnp.dot(g.astype(BF16), w_out_ref[lo:lo + te, :], preferred_element_type=F32)
    h_ref[0] = acc
    if emit_tail:
        tail_ref[...] = halo_ref[...]


def _conv_layer(x, halo_in, nw, w_in, cw, w_out, *, tm, emit_tail):
    bsz, t_len, d = x.shape
    out_shape = [jax.ShapeDtypeStruct((bsz, t_len, d), F32)]
    out_specs = [pl.BlockSpec((1, tm, d), lambda b, t: (b, t, 0))]
    if emit_tail:
        out_shape.append(jax.ShapeDtypeStruct((SUBLANES, E_CONV), F32))
        out_specs.append(pl.BlockSpec((SUBLANES, E_CONV), lambda b, t: (0, 0)))
    return pl.pallas_call(
        functools.partial(_conv_layer_kernel, te=min(CONV_TE, E_CONV), emit_tail=emit_tail),
        grid=(bsz, t_len // tm),
        in_specs=[
            pl.BlockSpec((1, tm, d), lambda b, t: (b, t, 0)),
            _resident(halo_in.shape),
            _resident(nw.shape),
            _resident(w_in.shape),
            _resident(cw.shape),
            _resident(w_out.shape),
        ],
        out_specs=out_specs,
        out_shape=out_shape,
        scratch_shapes=[pltpu.VMEM((SUBLANES, E_CONV), F32)],
        compiler_params=pltpu.CompilerParams(
            dimension_semantics=("arbitrary", "arbitrary"), vmem_limit_bytes=VMEM_LIMIT),
        name="conv_layer_meta" if emit_tail else "conv_layer",
    )(x, halo_in, nw, w_in, cw, w_out)


def _gate_preacts(u, wg_ref, gb_ref):
    return jnp.dot(u, wg_ref[...], preferred_element_type=F32) + gb_ref[...]


def _gate_columns(g):
    n = g.shape[0]
    lf = _log_sigmoid(g)
    hi = lf.astype(BF16).astype(F32)
    mid = (lf - hi).astype(BF16).astype(F32)
    low = lf - hi - mid
    lane = lax.broadcasted_iota(jnp.int32, (n, LANES), 1)
    terms = jnp.where(lane < N_GATES, hi, jnp.where(lane < 2 * N_GATES, mid, low)).astype(BF16)
    r = lax.broadcasted_iota(jnp.int32, (n, n), 0)
    c = lax.broadcasted_iota(jnp.int32, (n, n), 1)
    tril = (r >= c).astype(BF16)
    part = jnp.dot(tril, terms, preferred_element_type=F32)
    b = part + pltpu.roll(part, LANES - N_GATES, 1) + pltpu.roll(part, LANES - 2 * N_GATES, 1)
    return g.T, b, b.T


def _state_weights(li_r, b_r, b_last, m_prev):
    logw = b_last - b_r + li_r
    m_new = jnp.maximum(b_last + m_prev, jnp.max(logw, axis=1, keepdims=True))
    return m_new, jnp.exp(b_last + m_prev - m_new), jnp.exp(logw - m_new)


def _state_apply(c_ext, decay, w_r, kt, v):
    kw = kt * w_r
    c_new = decay * c_ext[:, :DV] + jnp.dot(kw.astype(BF16), v, preferred_element_type=F32)
    n_new = decay * c_ext[:, DV:DV + 1] + jnp.sum(kw, axis=1, keepdims=True)
    lane = lax.broadcasted_iota(jnp.int32, (DK, LANES), 1)
    return jnp.concatenate([c_new, jnp.where(lane == 0, n_new, 0.0)], axis=1)


def _k_transposed(wkt_ref, u, hd):
    return lax.dot_general(wkt_ref[hd], u, (((1,), (1,)), ((), ())), preferred_element_type=F32)


def _mlstm_layer_kernel(h_ref, c0_ref, m0_ref, nw_ref, w_ref, wkt_ref, wg_ref, gb_ref, hw_ref,
                        wo_ref, fw_ref, o_ref, c_ref, m_ref):
    @pl.when(pl.program_id(1) == 0)
    def _():
        c_ref[...] = c0_ref[...]
        m_ref[...] = m0_ref[...]

    n = MLSTM_L
    n_sub = h_ref.shape[1] // n
    r = lax.broadcasted_iota(jnp.int32, (n, n), 0)
    c = lax.broadcasted_iota(jnp.int32, (n, n), 1)
    causal = r >= c
    heads = range(N_HEADS)
    m_run = [m_ref[hd][0:1, 0:1] for hd in heads]

    def chunk_prep(ci):
        u = _rms_scale(h_ref[0, ci * n:(ci + 1) * n, :], nw_ref[...]).astype(BF16)
        g_t, b, b_t = _gate_columns(_gate_preacts(u, wg_ref, gb_ref))
        terms = []
        for hd in heads:
            li_r = g_t[hd:hd + 1, :]
            b_r = b_t[N_HEADS + hd:N_HEADS + hd + 1, :]
            b_c = b[:, N_HEADS + hd:N_HEADS + hd + 1]
            log_d = jnp.where(causal, b_c - b_r + li_r, -jnp.inf)
            log_inter = b_c + m_run[hd]
            m_row = jnp.maximum(log_inter, jnp.max(log_d, axis=1, keepdims=True))
            m_new, decay, w_r = _state_weights(li_r, b_r, b_c[n - 1:n, :], m_run[hd])
            terms.append((jnp.exp(log_d - m_row), jnp.exp(log_inter - m_row), jnp.exp(-m_row), decay, w_r))
            m_run[hd] = m_new
        return u, terms

    def proj(u, off, width, hd):
        return jnp.dot(u, w_ref[:, off + hd * width:off + (hd + 1) * width], preferred_element_type=F32)

    def qkv_proj(u, hd):
        q = (proj(u, 0, DK, hd) * (DK ** -0.5)).astype(BF16)
        return q, _k_transposed(wkt_ref, u, hd), proj(u, V_OFF, DV, hd).astype(BF16)

    n_items = n_sub * N_HEADS
    prep = [chunk_prep(0)]
    q, kt, v = qkv_proj(prep[0][0], 0)
    for item in range(n_items):
        ci, hd = divmod(item, N_HEADS)
        u, terms = prep[ci]
        d, inter, inv_floor, decay, w_r = terms[hd]
        if hd == 0:
            acc = h_ref[0, ci * n:(ci + 1) * n, :]
        c_ext = c_ref[hd]
        sd = jnp.dot(q, kt.astype(BF16), preferred_element_type=F32)
        o_gate = jax.nn.sigmoid(proj(u, O_OFF, DV, hd)) * hw_ref[:, hd * DV:(hd + 1) * DV]
        sd = sd * d
        intra = jnp.dot(sd.astype(BF16), v, preferred_element_type=F32)
        cross = jnp.dot(q, c_ext.astype(BF16), preferred_element_type=F32)
        z_pre = proj(u, Z_OFF, DV, hd)
        if hd == 1 and ci + 1 < n_sub:
            prep.append(chunk_prep(ci + 1))
        num = intra + inter * cross[:, :DV]
        den = jnp.sum(sd, axis=1, keepdims=True) + inter * cross[:, DV:DV + 1]
        hh = num * (1.0 / jnp.maximum(jnp.abs(den), inv_floor))
        hn = hh * lax.rsqrt(jnp.mean(hh * hh, axis=-1, keepdims=True) + RMS_EPS)
        c_ref[hd] = _state_apply(c_ext, decay, w_r, kt, v)
        if item + 1 < n_items:
            ci_next, hd_next = divmod(item + 1, N_HEADS)
            q, kt, v = qkv_proj(prep[ci_next][0], hd_next)
        gated = hn * o_gate * (z_pre * jax.nn.sigmoid(z_pre))
        acc = acc + jnp.dot(gated.astype(BF16), wo_ref[hd * DV:(hd + 1) * DV, :],
                            preferred_element_type=F32)
        if hd + 1 == N_HEADS:
            o_ref[0, ci * n:(ci + 1) * n, :] = _rms_scale(acc, fw_ref[...])
    for hd in heads:
        m_ref[hd] = jnp.broadcast_to(m_run[hd], (SUBLANES, LANES))


def _mlstm_layer(h1, c0, m0, nw, w, wkt, wg, gb, hw, wo, fw):
    bsz, t_len, d = h1.shape
    n_l = MLSTM_TM
    return pl.pallas_call(
        _mlstm_layer_kernel,
        grid=(bsz, t_len // n_l),
        in_specs=[pl.BlockSpec((1, n_l, d), lambda b, t: (b, t, 0))]
        + [_resident(a.shape) for a in (c0, m0, nw, w, wkt, wg, gb, hw, wo, fw)],
        out_specs=pl.BlockSpec((1, n_l, d), lambda b, t: (b, t, 0)),
        out_shape=jax.ShapeDtypeStruct((bsz, t_len, d), F32),
        scratch_shapes=[pltpu.VMEM((N_HEADS, DK, DV_EXT), F32),
                        pltpu.VMEM((N_HEADS, SUBLANES, LANES), F32)],
        compiler_params=pltpu.CompilerParams(
            dimension_semantics=("arbitrary", "arbitrary"), vmem_limit_bytes=VMEM_LIMIT),
        name="mlstm_layer",
    )(h1, c0, m0, nw, w, wkt, wg, gb, hw, wo, fw)


def _meta_state_kernel(h_ref, nw_ref, wv_ref, wkt_ref, wg_ref, gb_ref, c_ref, m_ref):
    h1 = h_ref[...]
    n = h1.shape[0]
    u = _rms_scale(h1, nw_ref[...]).astype(BF16)
    g_t, b, b_t = _gate_columns(_gate_preacts(u, wg_ref, gb_ref))
    for hd in range(N_HEADS):
        v = jnp.dot(u, wv_ref[:, hd * DV:(hd + 1) * DV], preferred_element_type=F32).astype(BF16)
        b_last = b[n - 1:n, N_HEADS + hd:N_HEADS + hd + 1]
        m_new, decay, w_r = _state_weights(g_t[hd:hd + 1, :], b_t[N_HEADS + hd:N_HEADS + hd + 1, :],
                                           b_last, jnp.zeros((1, 1), F32))
        c_ref[hd] = _state_apply(jnp.zeros((DK, DV_EXT), F32), decay, w_r, _k_transposed(wkt_ref, u, hd), v)
        m_ref[hd] = jnp.broadcast_to(m_new, (SUBLANES, LANES))


def _meta_state(h1_meta, nw, w, wkt, wg, gb):
    full = lambda a: pl.BlockSpec(a.shape, lambda i: (0,) * a.ndim)
    return pl.pallas_call(
        _meta_state_kernel,
        grid=(1,),
        in_specs=[full(h1_meta), full(nw),
                  pl.BlockSpec((D_MODEL, E_MLSTM), lambda i: (0, V_OFF // E_MLSTM)),
                  full(wkt), full(wg), full(gb)],
        out_specs=[pl.BlockSpec((N_HEADS, DK, DV_EXT), lambda i: (0, 0, 0)),
                   pl.BlockSpec((N_HEADS, SUBLANES, LANES), lambda i: (0, 0, 0))],
        out_shape=[jax.ShapeDtypeStruct((N_HEADS, DK, DV_EXT), F32),
                   jax.ShapeDtypeStruct((N_HEADS, SUBLANES, LANES), F32)],
        compiler_params=pltpu.CompilerParams(
            dimension_semantics=("arbitrary",), vmem_limit_bytes=VMEM_LIMIT),
        name="mlstm_meta_state",
    )(h1_meta, nw, w, wkt, wg, gb)


def kernel(x, meta_tokens, norm_w, conv_in_w, conv_w, conv_out_w, mlstm_in_w, mlstm_gate_b,
           mlstm_head_norm_w, mlstm_out_w, final_norm_w):
    w_in = conv_in_w[0].astype(BF16)
    cw = conv_w[0]
    w_out = conv_out_w[0].astype(BF16)
    mw = mlstm_in_w[0]
    w = mw.astype(BF16)
    wkt = mw[:, QK:2 * QK].reshape(D_MODEL, N_HEADS, DK).transpose(1, 2, 0).astype(BF16)
    reps = 3
    wg = jnp.pad(jnp.tile(mw[:, MLSTM_PROJ_W:], (1, reps)), ((0, 0), (0, LANES - reps * N_GATES))).astype(BF16)
    gb = jnp.pad(jnp.tile(mlstm_gate_b[0], reps), (0, LANES - reps * N_GATES)).reshape(1, LANES)
    hw = mlstm_head_norm_w[0].reshape(1, E_MLSTM)
    wo = mlstm_out_w[0].astype(BF16)
    nw0 = norm_w[0].reshape(1, D_MODEL)
    nw1 = norm_w[1].reshape(1, D_MODEL)
    fw = final_norm_w.reshape(1, D_MODEL)

    zero_halo = jnp.zeros((SUBLANES, E_CONV), F32)
    h1_meta, meta_tail = _conv_layer(meta_tokens[None], zero_halo, nw0, w_in, cw, w_out,
                                     tm=N_META, emit_tail=True)
    c0, m0 = _meta_state(h1_meta[0], nw1, w, wkt, wg, gb)

    (h1,) = _conv_layer(x, meta_tail, nw0, w_in, cw, w_out, tm=CONV_TM, emit_tail=False)
    return _mlstm_layer(h1, c0, m0, nw1, w, wkt, wg, gb, hw, wo, fw)
```

```python
import functools

import jax
import jax.numpy as jnp
from jax import lax
from jax.experimental import pallas as pl
from jax.experimental.pallas import tpu as pltpu

D_MODEL = 1024
N_META = 16
E_CONV = 2048
CONV_WIDTH = 3
E_MLSTM = 2048
N_HEADS = 4
DV = E_MLSTM // N_HEADS
DK = DV // 2
QK = N_HEADS * DK
N_GATES = 2 * N_HEADS
RMS_EPS = 1e-6

LANES = 128
SUBLANES = 8
DV_EXT = DV + LANES
V_OFF = 2 * QK
O_OFF = V_OFF + E_MLSTM
Z_OFF = O_OFF + E_MLSTM
MLSTM_PROJ_W = Z_OFF + E_MLSTM

CONV_TM = 1024
CONV_TE = 1024
MLSTM_L = 256
MLSTM_TM = 512
META_TE = 256
META_TC = 512
VMEM_LIMIT = 56 * 1024 * 1024

F32 = jnp.float32
BF16 = jnp.bfloat16


def _rms_scale(x, w_row):
    ms = jnp.mean(x * x, axis=-1, keepdims=True)
    return x * lax.rsqrt(ms + RMS_EPS) * w_row


def _log_sigmoid(x):
    return jnp.minimum(x, 0.0) - jnp.log1p(jnp.exp(-jnp.abs(x)))


def _resident(shape):
    nd = len(shape)
    return pl.BlockSpec(shape, lambda *_: (0,) * nd, pipeline_mode=pl.Buffered(1))


def _conv_chunk(p, halo, cw):
    b_gate, c_gate, xin, z = p
    tm, te = xin.shape
    row = lax.broadcasted_iota(jnp.int32, (tm, te), 0)
    cx = c_gate * xin
    prev1 = jnp.where(row == 0, halo[7:8, :], pltpu.roll(cx, 1, 0))
    prev2 = jnp.where(row == 0, halo[6:7, :], jnp.where(row == 1, halo[7:8, :], pltpu.roll(cx, 2, 0)))
    y = cw[0:1, :] * prev2 + cw[1:2, :] * prev1 + cw[2:3, :] * cx
    return (z * jax.nn.sigmoid(z)) * b_gate * y, cx[tm - SUBLANES:, :]


def _conv_layer_kernel(x_ref, halo_in_ref, nw_ref, wb_ref, wc_ref, wx_ref, wz_ref, cw_ref, w_out_ref,
                       h_ref, halo_ref, *, te):
    @pl.when(pl.program_id(1) == 0)
    def _():
        halo_ref[...] = halo_in_ref[...]

    x = x_ref[0]
    u = _rms_scale(x, nw_ref[...]).astype(BF16)
    n_chunks = E_CONV // te

    def in_proj(e):
        return [jnp.dot(u, w_ref[:, e * te:(e + 1) * te], preferred_element_type=F32)
                for w_ref in (wb_ref, wc_ref, wx_ref, wz_ref)]

    acc = x
    p_next = in_proj(0)
    for e in range(n_chunks):
        lo = e * te
        p = p_next
        if e + 1 < n_chunks:
            p_next = in_proj(e + 1)
        g, tail = _conv_chunk(p, halo_ref[:, lo:lo + te], cw_ref[:, lo:lo + te])
        halo_ref[:, lo:lo + te] = tail
        acc = acc + jnp.dot(g.astype(BF16), w_out_ref[lo:lo + te, :], preferred_element_type=F32)
    h_ref[0] = acc


def _conv_layer(x, halo_in, nw, w_groups, cw, w_out):
    bsz, t_len, d = x.shape
    tm = CONV_TM
    args = (halo_in, nw, *w_groups, cw, w_out)
    return pl.pallas_call(
        functools.partial(_conv_layer_kernel, te=CONV_TE),
        grid=(bsz, t_len // tm),
        in_specs=[pl.BlockSpec((1, tm, d), lambda b, t: (b, t, 0))] + [_resident(a.shape) for a in args],
        out_specs=pl.BlockSpec((1, tm, d), lambda b, t: (b, t, 0)),
        out_shape=jax.ShapeDtypeStruct((bsz, t_len, d), F32),
        scratch_shapes=[pltpu.VMEM((SUBLANES, E_CONV), F32)],
        compiler_params=pltpu.CompilerParams(
            dimension_semantics=("arbitrary", "arbitrary"), vmem_limit_bytes=VMEM_LIMIT),
        name="conv_layer",
    )(x, *args)


def _conv_meta_kernel(x_ref, nw_ref, wb_ref, wc_ref, wx_ref, wz_ref, cw_ref, w_out_ref,
                      wb_o, wc_o, wx_o, wz_o, w_out_o, h_ref, tail_ref):
    x = x_ref[...]
    u = _rms_scale(x, nw_ref[...]).astype(BF16)
    w = [r[...].astype(BF16) for r in (wb_ref, wc_ref, wx_ref, wz_ref)]
    for o_ref, wk in zip((wb_o, wc_o, wx_o, wz_o), w):
        o_ref[...] = wk
    w_out = w_out_ref[...].astype(BF16)
    w_out_o[...] = w_out
    p = [jnp.dot(u, wk, preferred_element_type=F32) for wk in w]
    g, tail = _conv_chunk(p, jnp.zeros((SUBLANES, p[0].shape[1]), F32), cw_ref[...])
    tail_ref[...] = tail
    part = jnp.dot(g.astype(BF16), w_out, preferred_element_type=F32)

    @pl.when(pl.program_id(0) == 0)
    def _():
        h_ref[...] = x + part

    @pl.when(pl.program_id(0) > 0)
    def _():
        h_ref[...] += part


def _conv_meta(meta, nw, conv_in_w, cw, conv_out_w):
    d, te = D_MODEL, META_TE
    n_e = E_CONV // te
    group = lambda k: pl.BlockSpec((d, te), lambda e: (0, k * n_e + e))
    col = pl.BlockSpec((d, te), lambda e: (0, e))
    full = lambda a: pl.BlockSpec(a.shape, lambda e: (0,) * a.ndim)
    outs = pl.pallas_call(
        _conv_meta_kernel,
        grid=(n_e,),
        in_specs=[full(meta), full(nw)] + [group(k) for k in range(4)]
        + [pl.BlockSpec((CONV_WIDTH, te), lambda e: (0, e)), pl.BlockSpec((te, d), lambda e: (e, 0))],
        out_specs=[col] * 4 + [pl.BlockSpec((te, d), lambda e: (e, 0)), full(meta),
                               pl.BlockSpec((SUBLANES, te), lambda e: (0, e))],
        out_shape=[jax.ShapeDtypeStruct((d, E_CONV), BF16)] * 4
        + [jax.ShapeDtypeStruct((E_CONV, d), BF16), jax.ShapeDtypeStruct(meta.shape, F32),
           jax.ShapeDtypeStruct((SUBLANES, E_CONV), F32)],
        compiler_params=pltpu.CompilerParams(
            dimension_semantics=("arbitrary",), vmem_limit_bytes=VMEM_LIMIT),
        name="conv_meta",
    )(meta, nw, conv_in_w, conv_in_w, conv_in_w, conv_in_w, cw, conv_out_w)
    return outs[:4], outs[4], outs[5], outs[6]


def _gate_preacts(u, wg_ref, gb_ref):
    return jnp.dot(u, wg_ref[...], preferred_element_type=F32) + gb_ref[...]


def _gate_columns(g):
    n = g.shape[0]
    lf = _log_sigmoid(g)
    hi = lf.astype(BF16).astype(F32)
    mid = (lf - hi).astype(BF16).astype(F32)
    low = lf - hi - mid
    lane = lax.broadcasted_iota(jnp.int32, (n, LANES), 1)
    terms = jnp.where(lane < N_GATES, hi, jnp.where(lane < 2 * N_GATES, mid, low)).astype(BF16)
    r = lax.broadcasted_iota(jnp.int32, (n, n), 0)
    c = lax.broadcasted_iota(jnp.int32, (n, n), 1)
    tril = (r >= c).astype(BF16)
    part = jnp.dot(tril, terms, preferred_element_type=F32)
    b = part + pltpu.roll(part, LANES - N_GATES, 1) + pltpu.roll(part, LANES - 2 * N_GATES, 1)
    return g.T, b, b.T


def _state_weights(li_r, b_r, b_last, m_prev):
    logw = b_last - b_r + li_r
    m_new = jnp.maximum(b_last + m_prev, jnp.max(logw, axis=1, keepdims=True))
    return m_new, jnp.exp(b_last + m_prev - m_new), jnp.exp(logw - m_new)


def _state_apply(c_ext, decay, w_r, kt, v):
    kw = kt * w_r
    c_new = decay * c_ext[:, :DV] + jnp.dot(kw.astype(BF16), v, preferred_element_type=F32)
    n_new = decay * c_ext[:, DV:DV + 1] + jnp.sum(kw, axis=1, keepdims=True)
    lane = lax.broadcasted_iota(jnp.int32, (DK, LANES), 1)
    return jnp.concatenate([c_new, jnp.where(lane == 0, n_new, 0.0)], axis=1)


def _k_transposed(wkt_ref, u, hd):
    return lax.dot_general(wkt_ref[hd], u, (((1,), (1,)), ((), ())), preferred_element_type=F32)


def _mlstm_layer_kernel(h_ref, c0_ref, m0_ref, nw_ref, w_ref, wkt_ref, wg_ref, gb_ref, hw_ref,
                        wo_ref, fw_ref, o_ref, c_ref, m_ref):
    @pl.when(pl.program_id(1) == 0)
    def _():
        c_ref[...] = c0_ref[...]
        m_ref[...] = m0_ref[...]

    n = MLSTM_L
    n_sub = h_ref.shape[1] // n
    r = lax.broadcasted_iota(jnp.int32, (n, n), 0)
    c = lax.broadcasted_iota(jnp.int32, (n, n), 1)
    causal = r >= c
    heads = range(N_HEADS)
    m_run = [m_ref[hd][0:1, 0:1] for hd in heads]

    def chunk_prep(ci):
        u = _rms_scale(h_ref[0, ci * n:(ci + 1) * n, :], nw_ref[...]).astype(BF16)
        g_t, b, b_t = _gate_columns(_gate_preacts(u, wg_ref, gb_ref))
        terms = []
        for hd in heads:
            li_r = g_t[hd:hd + 1, :]
            b_r = b_t[N_HEADS + hd:N_HEADS + hd + 1, :]
            b_c = b[:, N_HEADS + hd:N_HEADS + hd + 1]
            log_d = jnp.where(causal, b_c - b_r + li_r, -jnp.inf)
            log_inter = b_c + m_run[hd]
            m_row = jnp.maximum(log_inter, jnp.max(log_d, axis=1, keepdims=True))
            m_new, decay, w_r = _state_weights(li_r, b_r, b_c[n - 1:n, :], m_run[hd])
            terms.append((jnp.exp(log_d - m_row), jnp.exp(log_inter - m_row), jnp.exp(-m_row), decay, w_r))
            m_run[hd] = m_new
        return u, terms

    def proj(u, off, width, hd):
        return jnp.dot(u, w_ref[:, off + hd * width:off + (hd + 1) * width], preferred_element_type=F32)

    def qkv_proj(u, hd):
        q = (proj(u, 0, DK, hd) * (DK ** -0.5)).astype(BF16)
        return q, _k_transposed(wkt_ref, u, hd), proj(u, V_OFF, DV, hd).astype(BF16)

    n_items = n_sub * N_HEADS
    prep = [chunk_prep(0)]
    q, kt, v = qkv_proj(prep[0][0], 0)
    for item in range(n_items):
        ci, hd = divmod(item, N_HEADS)
        u, terms = prep[ci]
        d, inter, inv_floor, decay, w_r = terms[hd]
        if hd == 0:
            acc = h_ref[0, ci * n:(ci + 1) * n, :]
        c_ext = c_ref[hd]
        sd = jnp.dot(q, kt.astype(BF16), preferred_element_type=F32)
        o_gate = jax.nn.sigmoid(proj(u, O_OFF, DV, hd)) * hw_ref[:, hd * DV:(hd + 1) * DV]
        sd = sd * d
        intra = jnp.dot(sd.astype(BF16), v, preferred_element_type=F32)
        cross = jnp.dot(q, c_ext.astype(BF16), preferred_element_type=F32)
        z_pre = proj(u, Z_OFF, DV, hd)
        if hd == 1 and ci + 1 < n_sub:
            prep.append(chunk_prep(ci + 1))
        num = intra + inter * cross[:, :DV]
        den = jnp.sum(sd, axis=1, keepdims=True) + inter * cross[:, DV:DV + 1]
        hh = num * (1.0 / jnp.maximum(jnp.abs(den), inv_floor))
        hn = hh * lax.rsqrt(jnp.mean(hh * hh, axis=-1, keepdims=True) + RMS_EPS)
        c_ref[hd] = _state_apply(c_ext, decay, w_r, kt, v)
        if item + 1 < n_items:
            ci_next, hd_next = divmod(item + 1, N_HEADS)
            q, kt, v = qkv_proj(prep[ci_next][0], hd_next)
        gated = hn * o_gate * (z_pre * jax.nn.sigmoid(z_pre))
        acc = acc + jnp.dot(gated.astype(BF16), wo_ref[hd * DV:(hd + 1) * DV, :],
                            preferred_element_type=F32)
        if hd + 1 == N_HEADS:
            o_ref[0, ci * n:(ci + 1) * n, :] = _rms_scale(acc, fw_ref[...])
    for hd in heads:
        m_ref[hd] = jnp.broadcast_to(m_run[hd], (SUBLANES, LANES))


def _mlstm_layer(h1, c0, m0, nw, w, wkt, wg, gb, hw, wo, fw):
    bsz, t_len, d = h1.shape
    n_l = MLSTM_TM
    return pl.pallas_call(
        _mlstm_layer_kernel,
        grid=(bsz, t_len // n_l),
        in_specs=[pl.BlockSpec((1, n_l, d), lambda b, t: (b, t, 0))]
        + [_resident(a.shape) for a in (c0, m0, nw, w, wkt, wg, gb, hw, wo, fw)],
        out_specs=pl.BlockSpec((1, n_l, d), lambda b, t: (b, t, 0)),
        out_shape=jax.ShapeDtypeStruct((bsz, t_len, d), F32),
        scratch_shapes=[pltpu.VMEM((N_HEADS, DK, DV_EXT), F32),
                        pltpu.VMEM((N_HEADS, SUBLANES, LANES), F32)],
        compiler_params=pltpu.CompilerParams(
            dimension_semantics=("arbitrary", "arbitrary"), vmem_limit_bytes=VMEM_LIMIT),
        name="mlstm_layer",
    )(h1, c0, m0, nw, w, wkt, wg, gb, hw, wo, fw)


def _mlstm_meta_kernel(h_ref, nw_ref, w_ref, wo_ref, wkt_ref, wg_ref, gb_ref,
                       w_o, wo_o, c_ref, m_ref, k_scr, v_scr):
    j = pl.program_id(0)
    tc = w_ref.shape[1]
    wb = w_ref[...].astype(BF16)
    w_o[...] = wb
    wo_o[...] = wo_ref[...].astype(BF16)
    u = _rms_scale(h_ref[...], nw_ref[...]).astype(BF16)
    k0, v0, v1 = QK // tc, V_OFF // tc, O_OFF // tc

    @pl.when((j >= k0) & (j < v0))
    def _():
        k_scr[j - k0] = jnp.dot(u, wb, preferred_element_type=F32)

    @pl.when((j >= v0) & (j < v1))
    def _():
        v_scr[j - v0] = jnp.dot(u, wb, preferred_element_type=F32).astype(BF16)

    @pl.when(j == pl.num_programs(0) - 1)
    def _():
        n = u.shape[0]
        g_t, b, b_t = _gate_columns(_gate_preacts(u, wg_ref, gb_ref))
        for hd in range(N_HEADS):
            lo = hd * DK
            kt = k_scr[lo // tc][:, lo % tc:lo % tc + DK].T
            b_last = b[n - 1:n, N_HEADS + hd:N_HEADS + hd + 1]
            m_new, decay, w_r = _state_weights(g_t[hd:hd + 1, :], b_t[N_HEADS + hd:N_HEADS + hd + 1, :],
                                               b_last, jnp.zeros((1, 1), F32))
            c_ref[hd] = _state_apply(jnp.zeros((DK, DV_EXT), F32), decay, w_r, kt, v_scr[hd])
            m_ref[hd] = jnp.broadcast_to(m_new, (SUBLANES, LANES))


def _mlstm_meta(h1_meta, nw, mlstm_in_w, mlstm_out_w, wkt, wg, gb):
    d, tc = D_MODEL, META_TC
    n_j = MLSTM_PROJ_W // tc
    tr = E_MLSTM // n_j
    assert tc == DV
    full = lambda a: pl.BlockSpec(a.shape, lambda j: (0,) * a.ndim)
    return pl.pallas_call(
        _mlstm_meta_kernel,
        grid=(n_j,),
        in_specs=[full(h1_meta), full(nw), pl.BlockSpec((d, tc), lambda j: (0, j)),
                  pl.BlockSpec((tr, d), lambda j: (j, 0)), full(wkt), full(wg), full(gb)],
        out_specs=[pl.BlockSpec((d, tc), lambda j: (0, j)), pl.BlockSpec((tr, d), lambda j: (j, 0)),
                   pl.BlockSpec((N_HEADS, DK, DV_EXT), lambda j: (0, 0, 0)),
                   pl.BlockSpec((N_HEADS, SUBLANES, LANES), lambda j: (0, 0, 0))],
        out_shape=[jax.ShapeDtypeStruct((d, MLSTM_PROJ_W), BF16), jax.ShapeDtypeStruct((E_MLSTM, d), BF16),
                   jax.ShapeDtypeStruct((N_HEADS, DK, DV_EXT), F32),
                   jax.ShapeDtypeStruct((N_HEADS, SUBLANES, LANES), F32)],
        scratch_shapes=[pltpu.VMEM((QK // tc, N_META, tc), F32), pltpu.VMEM((N_HEADS, N_META, tc), BF16)],
        compiler_params=pltpu.CompilerParams(
            dimension_semantics=("arbitrary",), vmem_limit_bytes=VMEM_LIMIT),
        name="mlstm_meta",
    )(h1_meta, nw, mlstm_in_w, mlstm_out_w, wkt, wg, gb)


def kernel(x, meta_tokens, norm_w, conv_in_w, conv_w, conv_out_w, mlstm_in_w, mlstm_gate_b,
           mlstm_head_norm_w, mlstm_out_w, final_norm_w):
    cw = conv_w[0]
    mw = mlstm_in_w[0]
    wkt = mw[:, QK:2 * QK].reshape(D_MODEL, N_HEADS, DK).transpose(1, 2, 0).astype(BF16)
    reps = 3
    wg = jnp.pad(jnp.tile(mw[:, MLSTM_PROJ_W:], (1, reps)), ((0, 0), (0, LANES - reps * N_GATES))).astype(BF16)
    gb = jnp.pad(jnp.tile(mlstm_gate_b[0], reps), (0, LANES - reps * N_GATES)).reshape(1, LANES)
    hw = mlstm_head_norm_w[0].reshape(1, E_MLSTM)
    nw0 = norm_w[0].reshape(1, D_MODEL)
    nw1 = norm_w[1].reshape(1, D_MODEL)
    fw = final_norm_w.reshape(1, D_MODEL)

    w_groups, w_out, h1_meta, meta_tail = _conv_meta(meta_tokens, nw0, conv_in_w[0], cw, conv_out_w[0])
    w, wo, c0, m0 = _mlstm_meta(h1_meta, nw1, mw, mlstm_out_w[0], wkt, wg, gb)

    h1 = _conv_layer(x, meta_tail, nw0, w_groups, cw, w_out)
    return _mlstm_layer(h1, c0, m0, nw1, w, wkt, wg, gb, hw, wo, fw)
```

```python
import functools

import jax
import jax.numpy as jnp
from jax import lax
from jax.experimental import pallas as pl
from jax.experimental.pallas import tpu as pltpu

D_MODEL = 1024
N_META = 16
E_CONV = 2048
CONV_WIDTH = 3
E_MLSTM = 2048
N_HEADS = 4
DV = E_MLSTM // N_HEADS
DK = DV // 2
QK = N_HEADS * DK
N_GATES = 2 * N_HEADS
RMS_EPS = 1e-6

LANES = 128
SUBLANES = 8
DV_EXT = DV + LANES
V_OFF = 2 * QK
O_OFF = V_OFF + E_MLSTM
Z_OFF = O_OFF + E_MLSTM
MLSTM_PROJ_W = Z_OFF + E_MLSTM

CONV_TM = 1024
CONV_TE = 1024
MLSTM_L = 256
MLSTM_TM = 512
VMEM_LIMIT = 56 * 1024 * 1024
LANE_PAD = LANES

F32 = jnp.float32
BF16 = jnp.bfloat16


def _rms_scale(x, w_row):
    ms = jnp.mean(x * x, axis=-1, keepdims=True)
    return x * lax.rsqrt(ms + RMS_EPS) * w_row


def _log_sigmoid(x):
    return jnp.minimum(x, 0.0) - jnp.log1p(jnp.exp(-jnp.abs(x)))


def _resident(shape):
    nd = len(shape)
    return pl.BlockSpec(shape, lambda *_: (0,) * nd, pipeline_mode=pl.Buffered(1))


def _conv_layer_kernel(x_ref, halo_in_ref, nw_ref, w_in_ref, cw_ref, w_out_ref, *rest,
                       te, emit_tail):
    if emit_tail:
        h_ref, tail_ref, halo_ref = rest
    else:
        h_ref, halo_ref = rest

    @pl.when(pl.program_id(1) == 0)
    def _():
        halo_ref[...] = halo_in_ref[...]

    x = x_ref[0]
    tm = x.shape[0]
    u = _rms_scale(x, nw_ref[...]).astype(BF16)
    row = lax.broadcasted_iota(jnp.int32, (tm, te), 0)
    n_chunks = E_CONV // te

    def in_proj(e):
        return [jnp.dot(u, w_in_ref[:, k * E_CONV + e * te:k * E_CONV + (e + 1) * te],
                        preferred_element_type=F32) for k in range(4)]

    acc = x
    p_next = in_proj(0)
    for e in range(n_chunks):
        lo = e * te
        b_gate, c_gate, xin, z = p_next
        if e + 1 < n_chunks:
            p_next = in_proj(e + 1)
        cx = c_gate * xin
        halo = halo_ref[:, lo:lo + te]
        prev1 = jnp.where(row == 0, halo[7:8, :], pltpu.roll(cx, 1, 0))
        prev2 = jnp.where(row == 0, halo[6:7, :],
                          jnp.where(row == 1, halo[7:8, :], pltpu.roll(cx, 2, 0)))
        cw = cw_ref[:, lo:lo + te]
        y = cw[0:1, :] * prev2 + cw[1:2, :] * prev1 + cw[2:3, :] * cx
        halo_ref[:, lo:lo + te] = cx[tm - SUBLANES:, :]
        g = (z * jax.nn.sigmoid(z)) * b_gate * y
        acc = acc + jnp.dot(g.astype(BF16), w_out_ref[lo:lo + te, :D_MODEL], preferred_element_type=F32)
    h_ref[0] = acc
    if emit_tail:
        tail_ref[...] = halo_ref[...]


def _conv_layer(x, halo_in, nw, w_in, cw, w_out, *, tm, emit_tail):
    bsz, t_len, d = x.shape
    out_shape = [jax.ShapeDtypeStruct((bsz, t_len, d), F32)]
    out_specs = [pl.BlockSpec((1, tm, d), lambda b, t: (b, t, 0))]
    if emit_tail:
        out_shape.append(jax.ShapeDtypeStruct((SUBLANES, E_CONV), F32))
        out_specs.append(pl.BlockSpec((SUBLANES, E_CONV), lambda b, t: (0, 0)))
    return pl.pallas_call(
        functools.partial(_conv_layer_kernel, te=min(CONV_TE, E_CONV), emit_tail=emit_tail),
        grid=(bsz, t_len // tm),
        in_specs=[
            pl.BlockSpec((1, tm, d), lambda b, t: (b, t, 0)),
            _resident(halo_in.shape),
            _resident(nw.shape),
            _resident(w_in.shape),
            _resident(cw.shape),
            _resident(w_out.shape),
        ],
        out_specs=out_specs,
        out_shape=out_shape,
        scratch_shapes=[pltpu.VMEM((SUBLANES, E_CONV), F32)],
        compiler_params=pltpu.CompilerParams(
            dimension_semantics=("arbitrary", "arbitrary"), vmem_limit_bytes=VMEM_LIMIT),
        name="conv_layer_meta" if emit_tail else "conv_layer",
    )(x, halo_in, nw, w_in, cw, w_out)


def _gate_preacts(u, wg_ref, gb_ref):
    return jnp.dot(u, wg_ref[...], preferred_element_type=F32) + gb_ref[...]


def _gate_columns(g):
    n = g.shape[0]
    lf = _log_sigmoid(g)
    hi = lf.astype(BF16).astype(F32)
    mid = (lf - hi).astype(BF16).astype(F32)
    low = lf - hi - mid
    lane = lax.broadcasted_iota(jnp.int32, (n, LANES), 1)
    terms = jnp.where(lane < N_GATES, hi, jnp.where(lane < 2 * N_GATES, mid, low)).astype(BF16)
    r = lax.broadcasted_iota(jnp.int32, (n, n), 0)
    c = lax.broadcasted_iota(jnp.int32, (n, n), 1)
    tril = (r >= c).astype(BF16)
    part = jnp.dot(tril, terms, preferred_element_type=F32)
    b = part + pltpu.roll(part, LANES - N_GATES, 1) + pltpu.roll(part, LANES - 2 * N_GATES, 1)
    return g.T, b, b.T


def _state_weights(li_r, b_r, b_last, m_prev):
    logw = b_last - b_r + li_r
    m_new = jnp.maximum(b_last + m_prev, jnp.max(logw, axis=1, keepdims=True))
    return m_new, jnp.exp(b_last + m_prev - m_new), jnp.exp(logw - m_new)


def _state_apply(c_ext, decay, w_r, kt, v):
    kw = kt * w_r
    c_new = decay * c_ext[:, :DV] + jnp.dot(kw.astype(BF16), v, preferred_element_type=F32)
    n_new = decay * c_ext[:, DV:DV + 1] + jnp.sum(kw, axis=1, keepdims=True)
    lane = lax.broadcasted_iota(jnp.int32, (DK, LANES), 1)
    return jnp.concatenate([c_new, jnp.where(lane == 0, n_new, 0.0)], axis=1)


def _k_transposed(wkt_ref, u, hd):
    return lax.dot_general(wkt_ref[hd], u, (((1,), (1,)), ((), ())), preferred_element_type=F32)


def _mlstm_layer_kernel(h_ref, c0_ref, m0_ref, nw_ref, w_ref, wkt_ref, wg_ref, gb_ref, hw_ref,
                        wo_ref, fw_ref, o_ref, c_ref, m_ref):
    @pl.when(pl.program_id(1) == 0)
    def _():
        c_ref[...] = c0_ref[...]
        m_ref[...] = m0_ref[...]

    n = MLSTM_L
    n_sub = h_ref.shape[1] // n
    r = lax.broadcasted_iota(jnp.int32, (n, n), 0)
    c = lax.broadcasted_iota(jnp.int32, (n, n), 1)
    causal = r >= c
    heads = range(N_HEADS)
    m_run = [m_ref[hd][0:1, 0:1] for hd in heads]

    def chunk_prep(ci):
        u = _rms_scale(h_ref[0, ci * n:(ci + 1) * n, :], nw_ref[...]).astype(BF16)
        g_t, b, b_t = _gate_columns(_gate_preacts(u, wg_ref, gb_ref))
        terms = []
        for hd in heads:
            li_r = g_t[hd:hd + 1, :]
            b_r = b_t[N_HEADS + hd:N_HEADS + hd + 1, :]
            b_c = b[:, N_HEADS + hd:N_HEADS + hd + 1]
            log_d = jnp.where(causal, b_c - b_r + li_r, -jnp.inf)
            log_inter = b_c + m_run[hd]
            m_row = jnp.maximum(log_inter, jnp.max(log_d, axis=1, keepdims=True))
            m_new, decay, w_r = _state_weights(li_r, b_r, b_c[n - 1:n, :], m_run[hd])
            terms.append((jnp.exp(log_d - m_row), jnp.exp(log_inter - m_row), jnp.exp(-m_row), decay, w_r))
            m_run[hd] = m_new
        return u, terms

    def proj(u, off, width, hd):
        return jnp.dot(u, w_ref[:, off + hd * width:off + (hd + 1) * width], preferred_element_type=F32)

    def qkv_proj(u, hd):
        q = (proj(u, 0, DK, hd) * (DK ** -0.5)).astype(BF16)
        return q, _k_transposed(wkt_ref, u, hd), proj(u, V_OFF, DV, hd).astype(BF16)

    n_items = n_sub * N_HEADS
    prep = [chunk_prep(0)]
    q, kt, v = qkv_proj(prep[0][0], 0)
    for item in range(n_items):
        ci, hd = divmod(item, N_HEADS)
        u, terms = prep[ci]
        d, inter, inv_floor, decay, w_r = terms[hd]
        if hd == 0:
            acc = h_ref[0, ci * n:(ci + 1) * n, :]
        c_ext = c_ref[hd]
        sd = jnp.dot(q, kt.astype(BF16), preferred_element_type=F32)
        o_gate = jax.nn.sigmoid(proj(u, O_OFF, DV, hd)) * hw_ref[:, hd * DV:(hd + 1) * DV]
        sd = sd * d
        intra = jnp.dot(sd.astype(BF16), v, preferred_element_type=F32)
        cross = jnp.dot(q, c_ext.astype(BF16), preferred_element_type=F32)
        z_pre = proj(u, Z_OFF, DV, hd)
        if hd == 1 and ci + 1 < n_sub:
            prep.append(chunk_prep(ci + 1))
        num = intra + inter * cross[:, :DV]
        den = jnp.sum(sd, axis=1, keepdims=True) + inter * cross[:, DV:DV + 1]
        hh = num * (1.0 / jnp.maximum(jnp.abs(den), inv_floor))
        hn = hh * lax.rsqrt(jnp.mean(hh * hh, axis=-1, keepdims=True) + RMS_EPS)
        c_ref[hd] = _state_apply(c_ext, decay, w_r, kt, v)
        if item + 1 < n_items:
            ci_next, hd_next = divmod(item + 1, N_HEADS)
            q, kt, v = qkv_proj(prep[ci_next][0], hd_next)
        gated = hn * o_gate * (z_pre * jax.nn.sigmoid(z_pre))
        acc = acc + jnp.dot(gated.astype(BF16), wo_ref[hd * DV:(hd + 1) * DV, :],
                            preferred_element_type=F32)
        if hd + 1 == N_HEADS:
            o_ref[0, ci * n:(ci + 1) * n, :] = _rms_scale(acc, fw_ref[...])
    for hd in heads:
        m_ref[hd] = jnp.broadcast_to(m_run[hd], (SUBLANES, LANES))


def _mlstm_layer(h1, c0, m0, nw, w, wkt, wg, gb, hw, wo, fw):
    bsz, t_len, d = h1.shape
    n_l = MLSTM_TM
    return pl.pallas_call(
        _mlstm_layer_kernel,
        grid=(bsz, t_len // n_l),
        in_specs=[pl.BlockSpec((1, n_l, d), lambda b, t: (b, t, 0))]
        + [_resident(a.shape) for a in (c0, m0, nw, w, wkt, wg, gb, hw, wo, fw)],
        out_specs=pl.BlockSpec((1, n_l, d), lambda b, t: (b, t, 0)),
        out_shape=jax.ShapeDtypeStruct((bsz, t_len, d), F32),
        scratch_shapes=[pltpu.VMEM((N_HEADS, DK, DV_EXT), F32),
                        pltpu.VMEM((N_HEADS, SUBLANES, LANES), F32)],
        compiler_params=pltpu.CompilerParams(
            dimension_semantics=("arbitrary", "arbitrary"), vmem_limit_bytes=VMEM_LIMIT),
        name="mlstm_layer",
    )(h1, c0, m0, nw, w, wkt, wg, gb, hw, wo, fw)


def _meta_state_kernel(h_ref, nw_ref, wv_ref, wkt_ref, wg_ref, gb_ref, c_ref, m_ref):
    h1 = h_ref[...]
    n = h1.shape[0]
    u = _rms_scale(h1, nw_ref[...]).astype(BF16)
    g_t, b, b_t = _gate_columns(_gate_preacts(u, wg_ref, gb_ref))
    for hd in range(N_HEADS):
        v = jnp.dot(u, wv_ref[:, hd * DV:(hd + 1) * DV], preferred_element_type=F32).astype(BF16)
        b_last = b[n - 1:n, N_HEADS + hd:N_HEADS + hd + 1]
        m_new, decay, w_r = _state_weights(g_t[hd:hd + 1, :], b_t[N_HEADS + hd:N_HEADS + hd + 1, :],
                                           b_last, jnp.zeros((1, 1), F32))
        c_ref[hd] = _state_apply(jnp.zeros((DK, DV_EXT), F32), decay, w_r, _k_transposed(wkt_ref, u, hd), v)
        m_ref[hd] = jnp.broadcast_to(m_new, (SUBLANES, LANES))


def _meta_state(h1_meta, nw, w, wkt, wg, gb):
    full = lambda a: pl.BlockSpec(a.shape, lambda i: (0,) * a.ndim)
    return pl.pallas_call(
        _meta_state_kernel,
        grid=(1,),
        in_specs=[full(h1_meta), full(nw),
                  pl.BlockSpec((D_MODEL, E_MLSTM), lambda i: (0, V_OFF // E_MLSTM)),
                  full(wkt), full(wg), full(gb)],
        out_specs=[pl.BlockSpec((N_HEADS, DK, DV_EXT), lambda i: (0, 0, 0)),
                   pl.BlockSpec((N_HEADS, SUBLANES, LANES), lambda i: (0, 0, 0))],
        out_shape=[jax.ShapeDtypeStruct((N_HEADS, DK, DV_EXT), F32),
                   jax.ShapeDtypeStruct((N_HEADS, SUBLANES, LANES), F32)],
        compiler_params=pltpu.CompilerParams(
            dimension_semantics=("arbitrary",), vmem_limit_bytes=VMEM_LIMIT),
        name="mlstm_meta_state",
    )(h1_meta, nw, w, wkt, wg, gb)


def kernel(x, meta_tokens, norm_w, conv_in_w, conv_w, conv_out_w, mlstm_in_w, mlstm_gate_b,
           mlstm_head_norm_w, mlstm_out_w, final_norm_w):
    lane_pad = lambda a: jnp.pad(a.astype(BF16), ((0, 0), (0, LANE_PAD)))
    w_in = lane_pad(conv_in_w[0])
    cw = conv_w[0]
    w_out = lane_pad(conv_out_w[0])
    mw = mlstm_in_w[0]
    w = mw.astype(BF16)
    wkt = mw[:, QK:2 * QK].reshape(D_MODEL, N_HEADS, DK).transpose(1, 2, 0).astype(BF16)
    reps = 3
    wg = jnp.pad(jnp.tile(mw[:, MLSTM_PROJ_W:], (1, reps)), ((0, 0), (0, LANES - reps * N_GATES))).astype(BF16)
    gb = jnp.pad(jnp.tile(mlstm_gate_b[0], reps), (0, LANES - reps * N_GATES)).reshape(1, LANES)
    hw = mlstm_head_norm_w[0].reshape(1, E_MLSTM)
    wo = mlstm_out_w[0].astype(BF16)
    nw0 = norm_w[0].reshape(1, D_MODEL)
    nw1 = norm_w[1].reshape(1, D_MODEL)
    fw = final_norm_w.reshape(1, D_MODEL)

    zero_halo = jnp.zeros((SUBLANES, E_CONV), F32)
    h1_meta, meta_tail = _conv_layer(meta_tokens[None], zero_halo, nw0, w_in, cw, w_out,
                                     tm=N_META, emit_tail=True)
    c0, m0 = _meta_state(h1_meta[0], nw1, w, wkt, wg, gb)

    (h1,) = _conv_layer(x, meta_tail, nw0, w_in, cw, w_out, tm=CONV_TM, emit_tail=False)
    return _mlstm_layer(h1, c0, m0, nw1, w, wkt, wg, gb, hw, wo, fw)
```

```python
import functools

import jax
import jax.numpy as jnp
from jax import lax
from jax.experimental import pallas as pl
from jax.experimental.pallas import tpu as pltpu

D_MODEL = 1024
N_META = 16
E_CONV = 2048
CONV_WIDTH = 3
E_MLSTM = 2048
N_HEADS = 4
DV = E_MLSTM // N_HEADS
DK = DV // 2
QK = N_HEADS * DK
N_GATES = 2 * N_HEADS
RMS_EPS = 1e-6

LANES = 128
SUBLANES = 8
DV_EXT = DV + LANES
V_OFF = 2 * QK
O_OFF = V_OFF + E_MLSTM
Z_OFF = O_OFF + E_MLSTM
MLSTM_PROJ_W = Z_OFF + E_MLSTM

CONV_TM = 1024
CONV_TE = 1024
MLSTM_L = 256
MLSTM_TM = 1024
VMEM_LIMIT = 56 * 1024 * 1024
LANE_PAD = LANES

F32 = jnp.float32
BF16 = jnp.bfloat16


def _rms_scale(x, w_row):
    ms = jnp.mean(x * x, axis=-1, keepdims=True)
    return x * lax.rsqrt(ms + RMS_EPS) * w_row


def _log_sigmoid(x):
    return jnp.minimum(x, 0.0) - jnp.log1p(jnp.exp(-jnp.abs(x)))


def _resident(shape):
    nd = len(shape)
    return pl.BlockSpec(shape, lambda *_: (0,) * nd, pipeline_mode=pl.Buffered(1))


def _conv_layer_kernel(x_ref, halo_in_ref, nw_ref, w_in_ref, cw_ref, w_out_ref, *rest,
                       te, emit_tail):
    if emit_tail:
        h_ref, tail_ref, halo_ref = rest
    else:
        h_ref, halo_ref = rest

    @pl.when(pl.program_id(1) == 0)
    def _():
        halo_ref[...] = halo_in_ref[...]

    x = x_ref[0]
    tm = x.shape[0]
    u = _rms_scale(x, nw_ref[...]).astype(BF16)
    row = lax.broadcasted_iota(jnp.int32, (tm, te), 0)
    n_chunks = E_CONV // te

    def in_proj(e):
        return [jnp.dot(u, w_in_ref[:, k * E_CONV + e * te:k * E_CONV + (e + 1) * te],
                        preferred_element_type=F32) for k in range(4)]

    acc = x
    p_next = in_proj(0)
    for e in range(n_chunks):
        lo = e * te
        b_gate, c_gate, xin, z = p_next
        if e + 1 < n_chunks:
            p_next = in_proj(e + 1)
        cx = c_gate * xin
        halo = halo_ref[:, lo:lo + te]
        prev1 = jnp.where(row == 0, halo[7:8, :], pltpu.roll(cx, 1, 0))
        prev2 = jnp.where(row == 0, halo[6:7, :],
                          jnp.where(row == 1, halo[7:8, :], pltpu.roll(cx, 2, 0)))
        cw = cw_ref[:, lo:lo + te]
        y = cw[0:1, :] * prev2 + cw[1:2, :] * prev1 + cw[2:3, :] * cx
        halo_ref[:, lo:lo + te] = cx[tm - SUBLANES:, :]
        g = (z * jax.nn.sigmoid(z)) * b_gate * y
        acc = acc + jnp.dot(g.astype(BF16), w_out_ref[lo:lo + te, :D_MODEL], preferred_element_type=F32)
    h_ref[0] = acc
    if emit_tail:
        tail_ref[...] = halo_ref[...]


def _conv_layer(x, halo_in, nw, w_in, cw, w_out, *, tm, emit_tail):
    bsz, t_len, d = x.shape
    out_shape = [jax.ShapeDtypeStruct((bsz, t_len, d), F32)]
    out_specs = [pl.BlockSpec((1, tm, d), lambda b, t: (b, t, 0))]
    if emit_tail:
        out_shape.append(jax.ShapeDtypeStruct((SUBLANES, E_CONV), F32))
        out_specs.append(pl.BlockSpec((SUBLANES, E_CONV), lambda b, t: (0, 0)))
    return pl.pallas_call(
        functools.partial(_conv_layer_kernel, te=min(CONV_TE, E_CONV), emit_tail=emit_tail),
        grid=(bsz, t_len // tm),
        in_specs=[
            pl.BlockSpec((1, tm, d), lambda b, t: (b, t, 0)),
            _resident(halo_in.shape),
            _resident(nw.shape),
            _resident(w_in.shape),
            _resident(cw.shape),
            _resident(w_out.shape),
        ],
        out_specs=out_specs,
        out_shape=out_shape,
        scratch_shapes=[pltpu.VMEM((SUBLANES, E_CONV), F32)],
        compiler_params=pltpu.CompilerParams(
            dimension_semantics=("arbitrary", "arbitrary"), vmem_limit_bytes=VMEM_LIMIT),
        name="conv_layer_meta" if emit_tail else "conv_layer",
    )(x, halo_in, nw, w_in, cw, w_out)


def _gate_preacts(u, wg_ref, gb_ref):
    return jnp.dot(u, wg_ref[...], preferred_element_type=F32) + gb_ref[...]


def _gate_columns(g):
    n = g.shape[0]
    lf = _log_sigmoid(g)
    hi = lf.astype(BF16).astype(F32)
    mid = (lf - hi).astype(BF16).astype(F32)
    low = lf - hi - mid
    lane = lax.broadcasted_iota(jnp.int32, (n, LANES), 1)
    terms = jnp.where(lane < N_GATES, hi, jnp.where(lane < 2 * N_GATES, mid, low)).astype(BF16)
    r = lax.broadcasted_iota(jnp.int32, (n, n), 0)
    c = lax.broadcasted_iota(jnp.int32, (n, n), 1)
    tril = (r >= c).astype(BF16)
    part = jnp.dot(tril, terms, preferred_element_type=F32)
    b = part + pltpu.roll(part, LANES - N_GATES, 1) + pltpu.roll(part, LANES - 2 * N_GATES, 1)
    return g.T, b, b.T


def _state_weights(li_r, b_r, b_last, m_prev):
    logw = b_last - b_r + li_r
    m_new = jnp.maximum(b_last + m_prev, jnp.max(logw, axis=1, keepdims=True))
    return m_new, jnp.exp(b_last + m_prev - m_new), jnp.exp(logw - m_new)


def _state_apply(c_ext, decay, w_r, kt, v):
    kw = kt * w_r
    c_new = decay * c_ext[:, :DV] + jnp.dot(kw.astype(BF16), v, preferred_element_type=F32)
    n_new = decay * c_ext[:, DV:DV + 1] + jnp.sum(kw, axis=1, keepdims=True)
    lane = lax.broadcasted_iota(jnp.int32, (DK, LANES), 1)
    return jnp.concatenate([c_new, jnp.where(lane == 0, n_new, 0.0)], axis=1)


def _k_transposed(wkt_ref, u, hd):
    return lax.dot_general(wkt_ref[hd], u, (((1,), (1,)), ((), ())), preferred_element_type=F32)


def _mlstm_layer_kernel(h_ref, c0_ref, m0_ref, nw_ref, w_ref, wkt_ref, wg_ref, gb_ref, hw_ref,
                        wo_ref, fw_ref, o_ref, c_ref, m_ref):
    @pl.when(pl.program_id(1) == 0)
    def _():
        c_ref[...] = c0_ref[...]
        m_ref[...] = m0_ref[...]

    n = MLSTM_L
    n_sub = h_ref.shape[1] // n
    r = lax.broadcasted_iota(jnp.int32, (n, n), 0)
    c = lax.broadcasted_iota(jnp.int32, (n, n), 1)
    causal = r >= c
    heads = range(N_HEADS)
    m_run = [m_ref[hd][0:1, 0:1] for hd in heads]

    def chunk_prep(ci):
        u = _rms_scale(h_ref[0, ci * n:(ci + 1) * n, :], nw_ref[...]).astype(BF16)
        g_t, b, b_t = _gate_columns(_gate_preacts(u, wg_ref, gb_ref))
        terms = []
        for hd in heads:
            li_r = g_t[hd:hd + 1, :]
            b_r = b_t[N_HEADS + hd:N_HEADS + hd + 1, :]
            b_c = b[:, N_HEADS + hd:N_HEADS + hd + 1]
            log_d = jnp.where(causal, b_c - b_r + li_r, -jnp.inf)
            log_inter = b_c + m_run[hd]
            m_row = jnp.maximum(log_inter, jnp.max(log_d, axis=1, keepdims=True))
            m_new, decay, w_r = _state_weights(li_r, b_r, b_c[n - 1:n, :], m_run[hd])
            terms.append((jnp.exp(log_d - m_row), jnp.exp(log_inter - m_row), jnp.exp(-m_row), decay, w_r))
            m_run[hd] = m_new
        return u, terms

    def proj(u, off, width, hd):
        return jnp.dot(u, w_ref[:, off + hd * width:off + (hd + 1) * width], preferred_element_type=F32)

    def qkv_proj(u, hd):
        q = (proj(u, 0, DK, hd) * (DK ** -0.5)).astype(BF16)
        return q, _k_transposed(wkt_ref, u, hd), proj(u, V_OFF, DV, hd).astype(BF16)

    n_items = n_sub * N_HEADS
    prep = [chunk_prep(0)]
    q, kt, v = qkv_proj(prep[0][0], 0)
    for item in range(n_items):
        ci, hd = divmod(item, N_HEADS)
        u, terms = prep[ci]
        d, inter, inv_floor, decay, w_r = terms[hd]
        if hd == 0:
            acc = h_ref[0, ci * n:(ci + 1) * n, :]
        c_ext = c_ref[hd]
        sd = jnp.dot(q, kt.astype(BF16), preferred_element_type=F32)
        o_gate = jax.nn.sigmoid(proj(u, O_OFF, DV, hd)) * hw_ref[:, hd * DV:(hd + 1) * DV]
        sd = sd * d
        intra = jnp.dot(sd.astype(BF16), v, preferred_element_type=F32)
        cross = jnp.dot(q, c_ext.astype(BF16), preferred_element_type=F32)
        z_pre = proj(u, Z_OFF, DV, hd)
        if hd == 1 and ci + 1 < n_sub:
            prep.append(chunk_prep(ci + 1))
        num = intra + inter * cross[:, :DV]
        den = jnp.sum(sd, axis=1, keepdims=True) + inter * cross[:, DV:DV + 1]
        hh = num * (1.0 / jnp.maximum(jnp.abs(den), inv_floor))
        hn = hh * lax.rsqrt(jnp.mean(hh * hh, axis=-1, keepdims=True) + RMS_EPS)
        c_ref[hd] = _state_apply(c_ext, decay, w_r, kt, v)
        if item + 1 < n_items:
            ci_next, hd_next = divmod(item + 1, N_HEADS)
            q, kt, v = qkv_proj(prep[ci_next][0], hd_next)
        gated = hn * o_gate * (z_pre * jax.nn.sigmoid(z_pre))
        acc = acc + jnp.dot(gated.astype(BF16), wo_ref[hd * DV:(hd + 1) * DV, :],
                            preferred_element_type=F32)
        if hd + 1 == N_HEADS:
            o_ref[0, ci * n:(ci + 1) * n, :] = _rms_scale(acc, fw_ref[...])
    for hd in heads:
        m_ref[hd] = jnp.broadcast_to(m_run[hd], (SUBLANES, LANES))


def _mlstm_layer(h1, c0, m0, nw, w, wkt, wg, gb, hw, wo, fw):
    bsz, t_len, d = h1.shape
    n_l = MLSTM_TM
    return pl.pallas_call(
        _mlstm_layer_kernel,
        grid=(bsz, t_len // n_l),
        in_specs=[pl.BlockSpec((1, n_l, d), lambda b, t: (b, t, 0))]
        + [_resident(a.shape) for a in (c0, m0, nw, w, wkt, wg, gb, hw, wo, fw)],
        out_specs=pl.BlockSpec((1, n_l, d), lambda b, t: (b, t, 0)),
        out_shape=jax.ShapeDtypeStruct((bsz, t_len, d), F32),
        scratch_shapes=[pltpu.VMEM((N_HEADS, DK, DV_EXT), F32),
                        pltpu.VMEM((N_HEADS, SUBLANES, LANES), F32)],
        compiler_params=pltpu.CompilerParams(
            dimension_semantics=("arbitrary", "arbitrary"), vmem_limit_bytes=VMEM_LIMIT),
        name="mlstm_layer",
    )(h1, c0, m0, nw, w, wkt, wg, gb, hw, wo, fw)


def _meta_state_kernel(h_ref, nw_ref, wv_ref, wkt_ref, wg_ref, gb_ref, c_ref, m_ref):
    h1 = h_ref[...]
    n = h1.shape[0]
    u = _rms_scale(h1, nw_ref[...]).astype(BF16)
    g_t, b, b_t = _gate_columns(_gate_preacts(u, wg_ref, gb_ref))
    for hd in range(N_HEADS):
        v = jnp.dot(u, wv_ref[:, hd * DV:(hd + 1) * DV], preferred_element_type=F32).astype(BF16)
        b_last = b[n - 1:n, N_HEADS + hd:N_HEADS + hd + 1]
        m_new, decay, w_r = _state_weights(g_t[hd:hd + 1, :], b_t[N_HEADS + hd:N_HEADS + hd + 1, :],
                                           b_last, jnp.zeros((1, 1), F32))
        c_ref[hd] = _state_apply(jnp.zeros((DK, DV_EXT), F32), decay, w_r, _k_transposed(wkt_ref, u, hd), v)
        m_ref[hd] = jnp.broadcast_to(m_new, (SUBLANES, LANES))


def _meta_state(h1_meta, nw, w, wkt, wg, gb):
    full = lambda a: pl.BlockSpec(a.shape, lambda i: (0,) * a.ndim)
    return pl.pallas_call(
        _meta_state_kernel,
        grid=(1,),
        in_specs=[full(h1_meta), full(nw),
                  pl.BlockSpec((D_MODEL, E_MLSTM), lambda i: (0, V_OFF // E_MLSTM)),
                  full(wkt), full(wg), full(gb)],
        out_specs=[pl.BlockSpec((N_HEADS, DK, DV_EXT), lambda i: (0, 0, 0)),
                   pl.BlockSpec((N_HEADS, SUBLANES, LANES), lambda i: (0, 0, 0))],
        out_shape=[jax.ShapeDtypeStruct((N_HEADS, DK, DV_EXT), F32),
                   jax.ShapeDtypeStruct((N_HEADS, SUBLANES, LANES), F32)],
        compiler_params=pltpu.CompilerParams(
            dimension_semantics=("arbitrary",), vmem_limit_bytes=VMEM_LIMIT),
        name="mlstm_meta_state",
    )(h1_meta, nw, w, wkt, wg, gb)


def kernel(x, meta_tokens, norm_w, conv_in_w, conv_w, conv_out_w, mlstm_in_w, mlstm_gate_b,
           mlstm_head_norm_w, mlstm_out_w, final_norm_w):
    lane_pad = lambda a: jnp.pad(a.astype(BF16), ((0, 0), (0, LANE_PAD)))
    w_in = lane_pad(conv_in_w[0])
    cw = conv_w[0]
    w_out = lane_pad(conv_out_w[0])
    mw = mlstm_in_w[0]
    w = mw.astype(BF16)
    wkt = mw[:, QK:2 * QK].reshape(D_MODEL, N_HEADS, DK).transpose(1, 2, 0).astype(BF16)
    reps = 3
    wg = jnp.pad(jnp.tile(mw[:, MLSTM_PROJ_W:], (1, reps)), ((0, 0), (0, LANES - reps * N_GATES))).astype(BF16)
    gb = jnp.pad(jnp.tile(mlstm_gate_b[0], reps), (0, LANES - reps * N_GATES)).reshape(1, LANES)
    hw = mlstm_head_norm_w[0].reshape(1, E_MLSTM)
    wo = mlstm_out_w[0].astype(BF16)
    nw0 = norm_w[0].reshape(1, D_MODEL)
    nw1 = norm_w[1].reshape(1, D_MODEL)
    fw = final_norm_w.reshape(1, D_MODEL)

    zero_halo = jnp.zeros((SUBLANES, E_CONV), F32)
    h1_meta, meta_tail = _conv_layer(meta_tokens[None], zero_halo, nw0, w_in, cw, w_out,
                                     tm=N_META, emit_tail=True)
    c0, m0 = _meta_state(h1_meta[0], nw1, w, wkt, wg, gb)

    (h1,) = _conv_layer(x, meta_tail, nw0, w_in, cw, w_out, tm=CONV_TM, emit_tail=False)
    return _mlstm_layer(h1, c0, m0, nw1, w, wkt, wg, gb, hw, wo, fw)
```

```python
import functools

import jax
import jax.numpy as jnp
from jax import lax
from jax.experimental import pallas as pl
from jax.experimental.pallas import tpu as pltpu

D_MODEL = 1024
N_META = 16
E_CONV = 2048
CONV_WIDTH = 3
E_MLSTM = 2048
N_HEADS = 4
DV = E_MLSTM // N_HEADS
DK = DV // 2
QK = N_HEADS * DK
N_GATES = 2 * N_HEADS
RMS_EPS = 1e-6

LANES = 128
SUBLANES = 8
DV_EXT = DV + LANES
V_OFF = 2 * QK
O_OFF = V_OFF + E_MLSTM
Z_OFF = O_OFF + E_MLSTM
MLSTM_PROJ_W = Z_OFF + E_MLSTM

CONV_TM = 1024
CONV_TE = 1024
MLSTM_L = 256
MLSTM_TM = 512
VMEM_LIMIT = 56 * 1024 * 1024
LANE_PAD = LANES

F32 = jnp.float32
BF16 = jnp.bfloat16


def _rms_scale(x, w_row):
    ms = jnp.mean(x * x, axis=-1, keepdims=True)
    return x * lax.rsqrt(ms + RMS_EPS) * w_row


def _log_sigmoid(x):
    return jnp.minimum(x, 0.0) - jnp.log1p(jnp.exp(-jnp.abs(x)))


def _resident(shape):
    nd = len(shape)
    return pl.BlockSpec(shape, lambda *_: (0,) * nd, pipeline_mode=pl.Buffered(1))


def _conv_layer_kernel(x_ref, halo_in_ref, nw_ref, w_in_ref, cw_ref, w_out_ref, *rest,
                       te, emit_tail):
    if emit_tail:
        h_ref, tail_ref, halo_ref = rest
    else:
        h_ref, halo_ref = rest

    @pl.when(pl.program_id(1) == 0)
    def _():
        halo_ref[...] = halo_in_ref[...]

    x = x_ref[0]
    tm = x.shape[0]
    u = _rms_scale(x, nw_ref[...]).astype(BF16)
    row = lax.broadcasted_iota(jnp.int32, (tm, te), 0)
    n_chunks = E_CONV // te

    def in_proj(e):
        return [jnp.dot(u, w_in_ref[:, k * E_CONV + e * te:k * E_CONV + (e + 1) * te],
                        preferred_element_type=F32) for k in range(4)]

    acc = x
    p_next = in_proj(0)
    for e in range(n_chunks):
        lo = e * te
        b_gate, c_gate, xin, z = p_next
        if e + 1 < n_chunks:
            p_next = in_proj(e + 1)
        zb_gate = (z * jax.nn.sigmoid(z)) * b_gate
        cx = c_gate * xin
        halo = halo_ref[:, lo:lo + te]
        prev1 = jnp.where(row == 0, halo[7:8, :], pltpu.roll(cx, 1, 0))
        prev2 = jnp.where(row == 0, halo[6:7, :],
                          jnp.where(row == 1, halo[7:8, :], pltpu.roll(cx, 2, 0)))
        cw = cw_ref[:, lo:lo + te]
        y = cw[0:1, :] * prev2 + cw[1:2, :] * prev1 + cw[2:3, :] * cx
        halo_ref[:, lo:lo + te] = cx[tm - SUBLANES:, :]
        g = zb_gate * y
        acc = acc + jnp.dot(g.astype(BF16), w_out_ref[lo:lo + te, :D_MODEL], preferred_element_type=F32)
    h_ref[0] = acc
    if emit_tail:
        tail_ref[...] = halo_ref[...]


def _conv_layer(x, halo_in, nw, w_in, cw, w_out, *, tm, emit_tail):
    bsz, t_len, d = x.shape
    out_shape = [jax.ShapeDtypeStruct((bsz, t_len, d), F32)]
    out_specs = [pl.BlockSpec((1, tm, d), lambda b, t: (b, t, 0))]
    if emit_tail:
        out_shape.append(jax.ShapeDtypeStruct((SUBLANES, E_CONV), F32))
        out_specs.append(pl.BlockSpec((SUBLANES, E_CONV), lambda b, t: (0, 0)))
    return pl.pallas_call(
        functools.partial(_conv_layer_kernel, te=min(CONV_TE, E_CONV), emit_tail=emit_tail),
        grid=(bsz, t_len // tm),
        in_specs=[
            pl.BlockSpec((1, tm, d), lambda b, t: (b, t, 0)),
            _resident(halo_in.shape),
            _resident(nw.shape),
            _resident(w_in.shape),
            _resident(cw.shape),
            _resident(w_out.shape),
        ],
        out_specs=out_specs,
        out_shape=out_shape,
        scratch_shapes=[pltpu.VMEM((SUBLANES, E_CONV), F32)],
        compiler_params=pltpu.CompilerParams(
            dimension_semantics=("arbitrary", "arbitrary"), vmem_limit_bytes=VMEM_LIMIT),
        name="conv_layer_meta" if emit_tail else "conv_layer",
    )(x, halo_in, nw, w_in, cw, w_out)


def _gate_preacts(u, wg_ref, gb_ref):
    return jnp.dot(u, wg_ref[...], preferred_element_type=F32) + gb_ref[...]


def _gate_columns(g):
    n = g.shape[0]
    lf = _log_sigmoid(g)
    hi = lf.astype(BF16).astype(F32)
    mid = (lf - hi).astype(BF16).astype(F32)
    low = lf - hi - mid
    lane = lax.broadcasted_iota(jnp.int32, (n, LANES), 1)
    terms = jnp.where(lane < N_GATES, hi, jnp.where(lane < 2 * N_GATES, mid, low)).astype(BF16)
    r = lax.broadcasted_iota(jnp.int32, (n, n), 0)
    c = lax.broadcasted_iota(jnp.int32, (n, n), 1)
    tril = (r >= c).astype(BF16)
    part = jnp.dot(tril, terms, preferred_element_type=F32)
    b = part + pltpu.roll(part, LANES - N_GATES, 1) + pltpu.roll(part, LANES - 2 * N_GATES, 1)
    return g.T, b, b.T


def _state_weights(li_r, b_r, b_last, m_prev):
    logw = b_last - b_r + li_r
    m_new = jnp.maximum(b_last + m_prev, jnp.max(logw, axis=1, keepdims=True))
    return m_new, jnp.exp(b_last + m_prev - m_new), jnp.exp(logw - m_new)


def _state_apply(c_ext, decay, w_r, kt, v):
    kw = kt * w_r
    c_new = decay * c_ext[:, :DV] + jnp.dot(kw.astype(BF16), v, preferred_element_type=F32)
    n_new = decay * c_ext[:, DV:DV + 1] + jnp.sum(kw, axis=1, keepdims=True)
    lane = lax.broadcasted_iota(jnp.int32, (DK, LANES), 1)
    return jnp.concatenate([c_new, jnp.where(lane == 0, n_new, 0.0)], axis=1)


def _k_transposed(wkt_ref, u, hd):
    return lax.dot_general(wkt_ref[hd], u, (((1,), (1,)), ((), ())), preferred_element_type=F32)


def _mlstm_layer_kernel(h_ref, c0_ref, m0_ref, nw_ref, w_ref, wkt_ref, wg_ref, gb_ref, hw_ref,
                        wo_ref, fw_ref, o_ref, c_ref, m_ref):
    @pl.when(pl.program_id(1) == 0)
    def _():
        c_ref[...] = c0_ref[...]
        m_ref[...] = m0_ref[...]

    n = MLSTM_L
    n_sub = h_ref.shape[1] // n
    r = lax.broadcasted_iota(jnp.int32, (n, n), 0)
    c = lax.broadcasted_iota(jnp.int32, (n, n), 1)
    causal = r >= c
    heads = range(N_HEADS)
    m_run = [m_ref[hd][0:1, 0:1] for hd in heads]

    def chunk_prep(ci):
        u = _rms_scale(h_ref[0, ci * n:(ci + 1) * n, :], nw_ref[...]).astype(BF16)
        g_t, b, b_t = _gate_columns(_gate_preacts(u, wg_ref, gb_ref))
        terms = []
        for hd in heads:
            li_r = g_t[hd:hd + 1, :]
            b_r = b_t[N_HEADS + hd:N_HEADS + hd + 1, :]
            b_c = b[:, N_HEADS + hd:N_HEADS + hd + 1]
            log_d = jnp.where(causal, b_c - b_r + li_r, -jnp.inf)
            log_inter = b_c + m_run[hd]
            m_row = jnp.maximum(log_inter, jnp.max(log_d, axis=1, keepdims=True))
            m_new, decay, w_r = _state_weights(li_r, b_r, b_c[n - 1:n, :], m_run[hd])
            terms.append((jnp.exp(log_d - m_row), jnp.exp(log_inter - m_row), jnp.exp(-m_row), decay, w_r))
            m_run[hd] = m_new
        return u, terms

    def proj(u, off, width, hd):
        return jnp.dot(u, w_ref[:, off + hd * width:off + (hd + 1) * width], preferred_element_type=F32)

    def qkv_proj(u, hd):
        q = (proj(u, 0, DK, hd) * (DK ** -0.5)).astype(BF16)
        return q, _k_transposed(wkt_ref, u, hd), proj(u, V_OFF, DV, hd).astype(BF16)

    n_items = n_sub * N_HEADS
    prep = [chunk_prep(0)]
    q, kt, v = qkv_proj(prep[0][0], 0)
    for item in range(n_items):
        ci, hd = divmod(item, N_HEADS)
        u, terms = prep[ci]
        d, inter, inv_floor, decay, w_r = terms[hd]
        if hd == 0:
            acc = h_ref[0, ci * n:(ci + 1) * n, :]
        c_ext = c_ref[hd]
        sd = jnp.dot(q, kt.astype(BF16), preferred_element_type=F32)
        o_gate = jax.nn.sigmoid(proj(u, O_OFF, DV, hd)) * hw_ref[:, hd * DV:(hd + 1) * DV]
        sd = sd * d
        intra = jnp.dot(sd.astype(BF16), v, preferred_element_type=F32)
        cross = jnp.dot(q, c_ext.astype(BF16), preferred_element_type=F32)
        z_pre = proj(u, Z_OFF, DV, hd)
        oz_gate = o_gate * (z_pre * jax.nn.sigmoid(z_pre))
        if hd == 1 and ci + 1 < n_sub:
            prep.append(chunk_prep(ci + 1))
        num = intra + inter * cross[:, :DV]
        den = jnp.sum(sd, axis=1, keepdims=True) + inter * cross[:, DV:DV + 1]
        hh = num * (1.0 / jnp.maximum(jnp.abs(den), inv_floor))
        hn = hh * lax.rsqrt(jnp.mean(hh * hh, axis=-1, keepdims=True) + RMS_EPS)
        c_ref[hd] = _state_apply(c_ext, decay, w_r, kt, v)
        if item + 1 < n_items:
            ci_next, hd_next = divmod(item + 1, N_HEADS)
            q, kt, v = qkv_proj(prep[ci_next][0], hd_next)
        gated = hn * oz_gate
        acc = acc + jnp.dot(gated.astype(BF16), wo_ref[hd * DV:(hd + 1) * DV, :],
                            preferred_element_type=F32)
        if hd + 1 == N_HEADS:
            o_ref[0, ci * n:(ci + 1) * n, :] = _rms_scale(acc, fw_ref[...])
    for hd in heads:
        m_ref[hd] = jnp.broadcast_to(m_run[hd], (SUBLANES, LANES))


def _mlstm_layer(h1, c0, m0, nw, w, wkt, wg, gb, hw, wo, fw):
    bsz, t_len, d = h1.shape
    n_l = MLSTM_TM
    return pl.pallas_call(
        _mlstm_layer_kernel,
        grid=(bsz, t_len // n_l),
        in_specs=[pl.BlockSpec((1, n_l, d), lambda b, t: (b, t, 0))]
        + [_resident(a.shape) for a in (c0, m0, nw, w, wkt, wg, gb, hw, wo, fw)],
        out_specs=pl.BlockSpec((1, n_l, d), lambda b, t: (b, t, 0)),
        out_shape=jax.ShapeDtypeStruct((bsz, t_len, d), F32),
        scratch_shapes=[pltpu.VMEM((N_HEADS, DK, DV_EXT), F32),
                        pltpu.VMEM((N_HEADS, SUBLANES, LANES), F32)],
        compiler_params=pltpu.CompilerParams(
            dimension_semantics=("arbitrary", "arbitrary"), vmem_limit_bytes=VMEM_LIMIT),
        name="mlstm_layer",
    )(h1, c0, m0, nw, w, wkt, wg, gb, hw, wo, fw)


def _meta_state_kernel(h_ref, nw_ref, wv_ref, wkt_ref, wg_ref, gb_ref, c_ref, m_ref):
    h1 = h_ref[...]
    n = h1.shape[0]
    u = _rms_scale(h1, nw_ref[...]).astype(BF16)
    g_t, b, b_t = _gate_columns(_gate_preacts(u, wg_ref, gb_ref))
    for hd in range(N_HEADS):
        v = jnp.dot(u, wv_ref[:, hd * DV:(hd + 1) * DV], preferred_element_type=F32).astype(BF16)
        b_last = b[n - 1:n, N_HEADS + hd:N_HEADS + hd + 1]
        m_new, decay, w_r = _state_weights(g_t[hd:hd + 1, :], b_t[N_HEADS + hd:N_HEADS + hd + 1, :],
                                           b_last, jnp.zeros((1, 1), F32))
        c_ref[hd] = _state_apply(jnp.zeros((DK, DV_EXT), F32), decay, w_r, _k_transposed(wkt_ref, u, hd), v)
        m_ref[hd] = jnp.broadcast_to(m_new, (SUBLANES, LANES))


def _meta_state(h1_meta, nw, w, wkt, wg, gb):
    full = lambda a: pl.BlockSpec(a.shape, lambda i: (0,) * a.ndim)
    return pl.pallas_call(
        _meta_state_kernel,
        grid=(1,),
        in_specs=[full(h1_meta), full(nw),
                  pl.BlockSpec((D_MODEL, E_MLSTM), lambda i: (0, V_OFF // E_MLSTM)),
                  full(wkt), full(wg), full(gb)],
        out_specs=[pl.BlockSpec((N_HEADS, DK, DV_EXT), lambda i: (0, 0, 0)),
                   pl.BlockSpec((N_HEADS, SUBLANES, LANES), lambda i: (0, 0, 0))],
        out_shape=[jax.ShapeDtypeStruct((N_HEADS, DK, DV_EXT), F32),
                   jax.ShapeDtypeStruct((N_HEADS, SUBLANES, LANES), F32)],
        compiler_params=pltpu.CompilerParams(
            dimension_semantics=("arbitrary",), vmem_limit_bytes=VMEM_LIMIT),
        name="mlstm_meta_state",
    )(h1_meta, nw, w, wkt, wg, gb)


def kernel(x, meta_tokens, norm_w, conv_in_w, conv_w, conv_out_w, mlstm_in_w, mlstm_gate_b,
           mlstm_head_norm_w, mlstm_out_w, final_norm_w):
    lane_pad = lambda a: jnp.pad(a.astype(BF16), ((0, 0), (0, LANE_PAD)))
    w_in = lane_pad(conv_in_w[0])
    cw = conv_w[0]
    w_out = lane_pad(conv_out_w[0])
    mw = mlstm_in_w[0]
    w = mw.astype(BF16)
    wkt = mw[:, QK:2 * QK].reshape(D_MODEL, N_HEADS, DK).transpose(1, 2, 0).astype(BF16)
    reps = 3
    wg = jnp.pad(jnp.tile(mw[:, MLSTM_PROJ_W:], (1, reps)), ((0, 0), (0, LANES - reps * N_GATES))).astype(BF16)
    gb = jnp.pad(jnp.tile(mlstm_gate_b[0], reps), (0, LANES - reps * N_GATES)).reshape(1, LANES)
    hw = mlstm_head_norm_w[0].reshape(1, E_MLSTM)
    wo = mlstm_out_w[0].astype(BF16)
    nw0 = norm_w[0].reshape(1, D_MODEL)
    nw1 = norm_w[1].reshape(1, D_MODEL)
    fw = final_norm_w.reshape(1, D_MODEL)

    zero_halo = jnp.zeros((SUBLANES, E_CONV), F32)
    h1_meta, meta_tail = _conv_layer(meta_tokens[None], zero_halo, nw0, w_in, cw, w_out,
                                     tm=N_META, emit_tail=True)
    c0, m0 = _meta_state(h1_meta[0], nw1, w, wkt, wg, gb)

    (h1,) = _conv_layer(x, meta_tail, nw0, w_in, cw, w_out, tm=CONV_TM, emit_tail=False)
    return _mlstm_layer(h1, c0, m0, nw1, w, wkt, wg, gb, hw, wo, fw)
```

```python
import functools

import jax
import jax.numpy as jnp
from jax import lax
from jax.experimental import pallas as pl
from jax.experimental.pallas import tpu as pltpu

D_MODEL = 1024
N_META = 16
E_CONV = 2048
CONV_WIDTH = 3
E_MLSTM = 2048
N_HEADS = 4
DV = E_MLSTM // N_HEADS
DK = DV // 2
QK = N_HEADS * DK
N_GATES = 2 * N_HEADS
RMS_EPS = 1e-6

LANES = 128
SUBLANES = 8
DV_EXT = DV + LANES
V_OFF = 2 * QK
O_OFF = V_OFF + E_MLSTM
Z_OFF = O_OFF + E_MLSTM
MLSTM_PROJ_W = Z_OFF + E_MLSTM

CONV_TM = 1024
CONV_TE = 1024
MLSTM_L = 256
MLSTM_TM = 512
VMEM_LIMIT = 56 * 1024 * 1024
LANE_PAD = LANES

F32 = jnp.float32
BF16 = jnp.bfloat16


def _rms_scale(x, w_row):
    ms = jnp.mean(x * x, axis=-1, keepdims=True)
    return x * lax.rsqrt(ms + RMS_EPS) * w_row


def _log_sigmoid(x):
    return jnp.minimum(x, 0.0) - jnp.log1p(jnp.exp(-jnp.abs(x)))


def _resident(shape):
    nd = len(shape)
    return pl.BlockSpec(shape, lambda *_: (0,) * nd, pipeline_mode=pl.Buffered(1))


def _conv_layer_kernel(x_ref, halo_in_ref, nw_ref, w_in_ref, cw_ref, w_out_ref, *rest,
                       te, emit_tail):
    if emit_tail:
        h_ref, tail_ref, halo_ref = rest
    else:
        h_ref, halo_ref = rest

    @pl.when(pl.program_id(1) == 0)
    def _():
        halo_ref[...] = halo_in_ref[...]

    x = x_ref[0]
    tm = x.shape[0]
    u = _rms_scale(x, nw_ref[...]).astype(BF16)
    row = lax.broadcasted_iota(jnp.int32, (tm, te), 0)
    n_chunks = E_CONV // te

    def in_proj(e):
        return [jnp.dot(u, w_in_ref[:, k * E_CONV + e * te:k * E_CONV + (e + 1) * te],
                        preferred_element_type=F32) for k in range(4)]

    acc = x
    p_next = in_proj(0)
    for e in range(n_chunks):
        lo = e * te
        b_gate, c_gate, xin, z = p_next
        if e + 1 < n_chunks:
            p_next = in_proj(e + 1)
        zb_gate = (z * jax.nn.sigmoid(z)) * b_gate
        cx = c_gate * xin
        halo = halo_ref[:, lo:lo + te]
        prev1 = jnp.where(row == 0, halo[7:8, :], pltpu.roll(cx, 1, 0))
        prev2 = jnp.where(row == 0, halo[6:7, :],
                          jnp.where(row == 1, halo[7:8, :], pltpu.roll(cx, 2, 0)))
        cw = cw_ref[:, lo:lo + te]
        y = cw[0:1, :] * prev2 + cw[1:2, :] * prev1 + cw[2:3, :] * cx
        halo_ref[:, lo:lo + te] = cx[tm - SUBLANES:, :]
        g = zb_gate * y
        acc = acc + jnp.dot(g.astype(BF16), w_out_ref[lo:lo + te, :D_MODEL], preferred_element_type=F32)
    h_ref[0] = acc
    if emit_tail:
        tail_ref[...] = halo_ref[...]


def _conv_layer(x, halo_in, nw, w_in, cw, w_out, *, tm, emit_tail):
    bsz, t_len, d = x.shape
    out_shape = [jax.ShapeDtypeStruct((bsz, t_len, d), F32)]
    out_specs = [pl.BlockSpec((1, tm, d), lambda b, t: (b, t, 0))]
    if emit_tail:
        out_shape.append(jax.ShapeDtypeStruct((SUBLANES, E_CONV), F32))
        out_specs.append(pl.BlockSpec((SUBLANES, E_CONV), lambda b, t: (0, 0)))
    return pl.pallas_call(
        functools.partial(_conv_layer_kernel, te=min(CONV_TE, E_CONV), emit_tail=emit_tail),
        grid=(bsz, t_len // tm),
        in_specs=[
            pl.BlockSpec((1, tm, d), lambda b, t: (b, t, 0)),
            _resident(halo_in.shape),
            _resident(nw.shape),
            _resident(w_in.shape),
            _resident(cw.shape),
            _resident(w_out.shape),
        ],
        out_specs=out_specs,
        out_shape=out_shape,
        scratch_shapes=[pltpu.VMEM((SUBLANES, E_CONV), F32)],
        compiler_params=pltpu.CompilerParams(
            dimension_semantics=("arbitrary", "arbitrary"), vmem_limit_bytes=VMEM_LIMIT),
        name="conv_layer_meta" if emit_tail else "conv_layer",
    )(x, halo_in, nw, w_in, cw, w_out)


def _gate_preacts(u, wg_ref, gb_ref):
    return jnp.dot(u, wg_ref[...], preferred_element_type=F32) + gb_ref[...]


def _gate_columns(g):
    n = g.shape[0]
    lf = _log_sigmoid(g)
    hi = lf.astype(BF16).astype(F32)
    mid = (lf - hi).astype(BF16).astype(F32)
    low = lf - hi - mid
    lane = lax.broadcasted_iota(jnp.int32, (n, LANES), 1)
    terms = jnp.where(lane < N_GATES, hi, jnp.where(lane < 2 * N_GATES, mid, low)).astype(BF16)
    r = lax.broadcasted_iota(jnp.int32, (n, n), 0)
    c = lax.broadcasted_iota(jnp.int32, (n, n), 1)
    tril = (r >= c).astype(BF16)
    part = jnp.dot(tril, terms, preferred_element_type=F32)
    b = part + pltpu.roll(part, LANES - N_GATES, 1) + pltpu.roll(part, LANES - 2 * N_GATES, 1)
    return g.T, b, b.T


def _state_weights(li_r, b_r, b_last, m_prev):
    logw = b_last - b_r + li_r
    m_new = jnp.maximum(b_last + m_prev, jnp.max(logw, axis=1, keepdims=True))
    return m_new, jnp.exp(b_last + m_prev - m_new), jnp.exp(logw - m_new)


def _state_apply(c_ext, decay, w_r, kt, v):
    kw = kt * w_r
    c_new = decay * c_ext[:, :DV] + jnp.dot(kw.astype(BF16), v, preferred_element_type=F32)
    n_new = decay * c_ext[:, DV:DV + 1] + jnp.sum(kw, axis=1, keepdims=True)
    lane = lax.broadcasted_iota(jnp.int32, (DK, LANES), 1)
    return jnp.concatenate([c_new, jnp.where(lane == 0, n_new, 0.0)], axis=1)


def _k_transposed(wkt_ref, u, hd):
    return lax.dot_general(wkt_ref[hd], u, (((1,), (1,)), ((), ())), preferred_element_type=F32)


def _mlstm_layer_kernel(h_ref, c0_ref, m0_ref, nw_ref, w_ref, wkt_ref, wg_ref, gb_ref, hw_ref,
                        wo_ref, fw_ref, o_ref, c_ref, m_ref):
    @pl.when(pl.program_id(1) == 0)
    def _():
        c_ref[...] = c0_ref[...]
        m_ref[...] = m0_ref[...]

    n = MLSTM_L
    n_sub = h_ref.shape[1] // n
    r = lax.broadcasted_iota(jnp.int32, (n, n), 0)
    c = lax.broadcasted_iota(jnp.int32, (n, n), 1)
    causal = r >= c
    heads = range(N_HEADS)
    m_run = [m_ref[hd][0:1, 0:1] for hd in heads]

    def chunk_prep(ci):
        u = _rms_scale(h_ref[0, ci * n:(ci + 1) * n, :], nw_ref[...]).astype(BF16)
        g_t, b, b_t = _gate_columns(_gate_preacts(u, wg_ref, gb_ref))
        terms = []
        for hd in heads:
            li_r = g_t[hd:hd + 1, :]
            b_r = b_t[N_HEADS + hd:N_HEADS + hd + 1, :]
            b_c = b[:, N_HEADS + hd:N_HEADS + hd + 1]
            log_d = jnp.where(causal, b_c - b_r + li_r, -jnp.inf)
            log_inter = b_c + m_run[hd]
            m_row = jnp.maximum(log_inter, jnp.max(log_d, axis=1, keepdims=True))
            m_new, decay, w_r = _state_weights(li_r, b_r, b_c[n - 1:n, :], m_run[hd])
            terms.append((jnp.exp(log_d - m_row), jnp.exp(log_inter - m_row), jnp.exp(-m_row), decay, w_r))
            m_run[hd] = m_new
        return u, terms

    def proj(u, off, width, hd):
        return jnp.dot(u, w_ref[:, off + hd * width:off + (hd + 1) * width], preferred_element_type=F32)

    def qkv_proj(u, hd):
        q = (proj(u, 0, DK, hd) * (DK ** -0.5)).astype(BF16)
        return q, _k_transposed(wkt_ref, u, hd), proj(u, V_OFF, DV, hd).astype(BF16)

    n_items = n_sub * N_HEADS
    prep = [chunk_prep(0)]
    q, kt, v = qkv_proj(prep[0][0], 0)
    for item in range(n_items):
        ci, hd = divmod(item, N_HEADS)
        u, terms = prep[ci]
        d, inter, inv_floor, decay, w_r = terms[hd]
        if hd == 0:
            acc = h_ref[0, ci * n:(ci + 1) * n, :]
        c_ext = c_ref[hd]
        sd = jnp.dot(q, kt.astype(BF16), preferred_element_type=F32)
        o_gate = jax.nn.sigmoid(proj(u, O_OFF, DV, hd)) * hw_ref[:, hd * DV:(hd + 1) * DV]
        z_pre = proj(u, Z_OFF, DV, hd)
        oz_gate = o_gate * (z_pre * jax.nn.sigmoid(z_pre))
        sd = sd * d
        intra = jnp.dot(sd.astype(BF16), v, preferred_element_type=F32)
        cross = jnp.dot(q, c_ext.astype(BF16), preferred_element_type=F32)
        if hd == 1 and ci + 1 < n_sub:
            prep.append(chunk_prep(ci + 1))
        num = intra + inter * cross[:, :DV]
        den = jnp.sum(sd, axis=1, keepdims=True) + inter * cross[:, DV:DV + 1]
        inv = 1.0 / jnp.maximum(jnp.abs(den), inv_floor)
        hn = num * (inv * lax.rsqrt(inv * inv * jnp.mean(num * num, axis=-1, keepdims=True) + RMS_EPS))
        c_ref[hd] = _state_apply(c_ext, decay, w_r, kt, v)
        if item + 1 < n_items:
            ci_next, hd_next = divmod(item + 1, N_HEADS)
            q, kt, v = qkv_proj(prep[ci_next][0], hd_next)
        gated = hn * oz_gate
        acc = acc + jnp.dot(gated.astype(BF16), wo_ref[hd * DV:(hd + 1) * DV, :],
                            preferred_element_type=F32)
        if hd + 1 == N_HEADS:
            o_ref[0, ci * n:(ci + 1) * n, :] = _rms_scale(acc, fw_ref[...])
    for hd in heads:
        m_ref[hd] = jnp.broadcast_to(m_run[hd], (SUBLANES, LANES))


def _mlstm_layer(h1, c0, m0, nw, w, wkt, wg, gb, hw, wo, fw):
    bsz, t_len, d = h1.shape
    n_l = MLSTM_TM
    return pl.pallas_call(
        _mlstm_layer_kernel,
        grid=(bsz, t_len // n_l),
        in_specs=[pl.BlockSpec((1, n_l, d), lambda b, t: (b, t, 0))]
        + [_resident(a.shape) for a in (c0, m0, nw, w, wkt, wg, gb, hw, wo, fw)],
        out_specs=pl.BlockSpec((1, n_l, d), lambda b, t: (b, t, 0)),
        out_shape=jax.ShapeDtypeStruct((bsz, t_len, d), F32),
        scratch_shapes=[pltpu.VMEM((N_HEADS, DK, DV_EXT), F32),
                        pltpu.VMEM((N_HEADS, SUBLANES, LANES), F32)],
        compiler_params=pltpu.CompilerParams(
            dimension_semantics=("arbitrary", "arbitrary"), vmem_limit_bytes=VMEM_LIMIT),
        name="mlstm_layer",
    )(h1, c0, m0, nw, w, wkt, wg, gb, hw, wo, fw)


def _meta_state_kernel(h_ref, nw_ref, wv_ref, wkt_ref, wg_ref, gb_ref, c_ref, m_ref):
    h1 = h_ref[...]
    n = h1.shape[0]
    u = _rms_scale(h1, nw_ref[...]).astype(BF16)
    g_t, b, b_t = _gate_columns(_gate_preacts(u, wg_ref, gb_ref))
    for hd in range(N_HEADS):
        v = jnp.dot(u, wv_ref[:, hd * DV:(hd + 1) * DV], preferred_element_type=F32).astype(BF16)
        b_last = b[n - 1:n, N_HEADS + hd:N_HEADS + hd + 1]
        m_new, decay, w_r = _state_weights(g_t[hd:hd + 1, :], b_t[N_HEADS + hd:N_HEADS + hd + 1, :],
                                           b_last, jnp.zeros((1, 1), F32))
        c_ref[hd] = _state_apply(jnp.zeros((DK, DV_EXT), F32), decay, w_r, _k_transposed(wkt_ref, u, hd), v)
        m_ref[hd] = jnp.broadcast_to(m_new, (SUBLANES, LANES))


def _meta_state(h1_meta, nw, w, wkt, wg, gb):
    full = lambda a: pl.BlockSpec(a.shape, lambda i: (0,) * a.ndim)
    return pl.pallas_call(
        _meta_state_kernel,
        grid=(1,),
        in_specs=[full(h1_meta), full(nw),
                  pl.BlockSpec((D_MODEL, E_MLSTM), lambda i: (0, V_OFF // E_MLSTM)),
                  full(wkt), full(wg), full(gb)],
        out_specs=[pl.BlockSpec((N_HEADS, DK, DV_EXT), lambda i: (0, 0, 0)),
                   pl.BlockSpec((N_HEADS, SUBLANES, LANES), lambda i: (0, 0, 0))],
        out_shape=[jax.ShapeDtypeStruct((N_HEADS, DK, DV_EXT), F32),
                   jax.ShapeDtypeStruct((N_HEADS, SUBLANES, LANES), F32)],
        compiler_params=pltpu.CompilerParams(
            dimension_semantics=("arbitrary",), vmem_limit_bytes=VMEM_LIMIT),
        name="mlstm_meta_state",
    )(h1_meta, nw, w, wkt, wg, gb)


def kernel(x, meta_tokens, norm_w, conv_in_w, conv_w, conv_out_w, mlstm_in_w, mlstm_gate_b,
           mlstm_head_norm_w, mlstm_out_w, final_norm_w):
    lane_pad = lambda a: jnp.pad(a.astype(BF16), ((0, 0), (0, LANE_PAD)))
    w_in = lane_pad(conv_in_w[0])
    cw = conv_w[0]
    w_out = lane_pad(conv_out_w[0])
    mw = mlstm_in_w[0]
    w = mw.astype(BF16)
    wkt = mw[:, QK:2 * QK].reshape(D_MODEL, N_HEADS, DK).transpose(1, 2, 0).astype(BF16)
    reps = 3
    wg = jnp.pad(jnp.tile(mw[:, MLSTM_PROJ_W:], (1, reps)), ((0, 0), (0, LANES - reps * N_GATES))).astype(BF16)
    gb = jnp.pad(jnp.tile(mlstm_gate_b[0], reps), (0, LANES - reps * N_GATES)).reshape(1, LANES)
    hw = mlstm_head_norm_w[0].reshape(1, E_MLSTM)
    wo = mlstm_out_w[0].astype(BF16)
    nw0 = norm_w[0].reshape(1, D_MODEL)
    nw1 = norm_w[1].reshape(1, D_MODEL)
    fw = final_norm_w.reshape(1, D_MODEL)

    zero_halo = jnp.zeros((SUBLANES, E_CONV), F32)
    h1_meta, meta_tail = _conv_layer(meta_tokens[None], zero_halo, nw0, w_in, cw, w_out,
                                     tm=N_META, emit_tail=True)
    c0, m0 = _meta_state(h1_meta[0], nw1, w, wkt, wg, gb)

    (h1,) = _conv_layer(x, meta_tail, nw0, w_in, cw, w_out, tm=CONV_TM, emit_tail=False)
    return _mlstm_layer(h1, c0, m0, nw1, w, wkt, wg, gb, hw, wo, fw)
```

```python
import functools

import jax
import jax.numpy as jnp
from jax import lax
from jax.experimental import pallas as pl
from jax.experimental.pallas import tpu as pltpu

D_MODEL = 1024
N_META = 16
E_CONV = 2048
CONV_WIDTH = 3
E_MLSTM = 2048
N_HEADS = 4
DV = E_MLSTM // N_HEADS
DK = DV // 2
QK = N_HEADS * DK
N_GATES = 2 * N_HEADS
RMS_EPS = 1e-6

LANES = 128
SUBLANES = 8
DV_EXT = DV + LANES
V_OFF = 2 * QK
O_OFF = V_OFF + E_MLSTM
Z_OFF = O_OFF + E_MLSTM
MLSTM_PROJ_W = Z_OFF + E_MLSTM

CONV_TM = 1024
CONV_TE = 1024
MLSTM_L = 256
MLSTM_TM = 512
VMEM_LIMIT = 56 * 1024 * 1024
LANE_PAD = LANES

F32 = jnp.float32
BF16 = jnp.bfloat16


def _rms_scale(x, w_row):
    ms = jnp.mean(x * x, axis=-1, keepdims=True)
    return x * lax.rsqrt(ms + RMS_EPS) * w_row


def _log_sigmoid(x):
    return jnp.minimum(x, 0.0) - jnp.log1p(jnp.exp(-jnp.abs(x)))


def _resident(shape):
    nd = len(shape)
    return pl.BlockSpec(shape, lambda *_: (0,) * nd, pipeline_mode=pl.Buffered(1))


def _conv_layer_kernel(x_ref, halo_in_ref, nw_ref, w_in_ref, cw_ref, w_out_ref, *rest,
                       te, emit_tail):
    if emit_tail:
        h_ref, tail_ref, halo_ref = rest
    else:
        h_ref, halo_ref = rest

    @pl.when(pl.program_id(1) == 0)
    def _():
        halo_ref[...] = halo_in_ref[...]

    x = x_ref[0]
    tm = x.shape[0]
    u = _rms_scale(x, nw_ref[...]).astype(BF16)
    row = lax.broadcasted_iota(jnp.int32, (tm, te), 0)
    n_chunks = E_CONV // te

    def in_proj(e):
        return [jnp.dot(u, w_in_ref[:, k * E_CONV + e * te:k * E_CONV + (e + 1) * te],
                        preferred_element_type=F32) for k in range(4)]

    acc = x
    p_next = in_proj(0)
    for e in range(n_chunks):
        lo = e * te
        b_gate, c_gate, xin, z = p_next
        if e + 1 < n_chunks:
            p_next = in_proj(e + 1)
        zb_gate = (z * jax.nn.sigmoid(z)) * b_gate
        cx = c_gate * xin
        halo = halo_ref[:, lo:lo + te]
        prev1 = jnp.where(row == 0, halo[7:8, :], pltpu.roll(cx, 1, 0))
        prev2 = jnp.where(row == 0, halo[6:7, :],
                          jnp.where(row == 1, halo[7:8, :], pltpu.roll(cx, 2, 0)))
        cw = cw_ref[:, lo:lo + te]
        y = cw[0:1, :] * prev2 + cw[1:2, :] * prev1 + cw[2:3, :] * cx
        halo_ref[:, lo:lo + te] = cx[tm - SUBLANES:, :]
        g = zb_gate * y
        acc = acc + jnp.dot(g.astype(BF16), w_out_ref[lo:lo + te, :D_MODEL], preferred_element_type=F32)
    h_ref[0] = acc
    if emit_tail:
        tail_ref[...] = halo_ref[...]


def _conv_layer(x, halo_in, nw, w_in, cw, w_out, *, tm, emit_tail):
    bsz, t_len, d = x.shape
    out_shape = [jax.ShapeDtypeStruct((bsz, t_len, d), F32)]
    out_specs = [pl.BlockSpec((1, tm, d), lambda b, t: (b, t, 0))]
    if emit_tail:
        out_shape.append(jax.ShapeDtypeStruct((SUBLANES, E_CONV), F32))
        out_specs.append(pl.BlockSpec((SUBLANES, E_CONV), lambda b, t: (0, 0)))
    return pl.pallas_call(
        functools.partial(_conv_layer_kernel, te=min(CONV_TE, E_CONV), emit_tail=emit_tail),
        grid=(bsz, t_len // tm),
        in_specs=[
            pl.BlockSpec((1, tm, d), lambda b, t: (b, t, 0)),
            _resident(halo_in.shape),
            _resident(nw.shape),
            _resident(w_in.shape),
            _resident(cw.shape),
            _resident(w_out.shape),
        ],
        out_specs=out_specs,
        out_shape=out_shape,
        scratch_shapes=[pltpu.VMEM((SUBLANES, E_CONV), F32)],
        compiler_params=pltpu.CompilerParams(
            dimension_semantics=("arbitrary", "arbitrary"), vmem_limit_bytes=VMEM_LIMIT),
        name="conv_layer_meta" if emit_tail else "conv_layer",
    )(x, halo_in, nw, w_in, cw, w_out)


def _gate_preacts(u, wg_ref, gb_ref):
    return jnp.dot(u, wg_ref[...], preferred_element_type=F32) + gb_ref[...]


def _gate_columns(g):
    n = g.shape[0]
    lf = _log_sigmoid(g)
    hi = lf.astype(BF16).astype(F32)
    mid = (lf - hi).astype(BF16).astype(F32)
    low = lf - hi - mid
    lane = lax.broadcasted_iota(jnp.int32, (n, LANES), 1)
    terms = jnp.where(lane < N_GATES, hi, jnp.where(lane < 2 * N_GATES, mid, low)).astype(BF16)
    r = lax.broadcasted_iota(jnp.int32, (n, n), 0)
    c = lax.broadcasted_iota(jnp.int32, (n, n), 1)
    tril = (r >= c).astype(BF16)
    part = jnp.dot(tril, terms, preferred_element_type=F32)
    b = part + pltpu.roll(part, LANES - N_GATES, 1) + pltpu.roll(part, LANES - 2 * N_GATES, 1)
    return g.T, b, b.T


def _state_weights(li_r, b_r, b_last, m_prev):
    logw = b_last - b_r + li_r
    m_new = jnp.maximum(b_last + m_prev, jnp.max(logw, axis=1, keepdims=True))
    return m_new, jnp.exp(b_last + m_prev - m_new), jnp.exp(logw - m_new)


def _state_apply(c_ext, decay, w_r, kt, v):
    kw = kt * w_r
    c_new = decay * c_ext[:, :DV] + jnp.dot(kw.astype(BF16), v, preferred_element_type=F32)
    n_new = decay * c_ext[:, DV:DV + 1] + jnp.sum(kw, axis=1, keepdims=True)
    lane = lax.broadcasted_iota(jnp.int32, (DK, LANES), 1)
    return jnp.concatenate([c_new, jnp.where(lane == 0, n_new, 0.0)], axis=1)


def _k_transposed(wkt_ref, u, hd):
    return lax.dot_general(wkt_ref[hd], u, (((1,), (1,)), ((), ())), preferred_element_type=F32)


def _mlstm_layer_kernel(h_ref, c0_ref, m0_ref, nw_ref, w_ref, wkt_ref, wg_ref, gb_ref, hw_ref,
                        wo_ref, fw_ref, o_ref, c_ref, m_ref):
    @pl.when(pl.program_id(1) == 0)
    def _():
        c_ref[...] = c0_ref[...]
        m_ref[...] = m0_ref[...]

    n = MLSTM_L
    n_sub = h_ref.shape[1] // n
    r = lax.broadcasted_iota(jnp.int32, (n, n), 0)
    c = lax.broadcasted_iota(jnp.int32, (n, n), 1)
    causal = r >= c
    heads = range(N_HEADS)
    m_run = [m_ref[hd][0:1, 0:1] for hd in heads]

    def chunk_prep(ci):
        u = _rms_scale(h_ref[0, ci * n:(ci + 1) * n, :], nw_ref[...]).astype(BF16)
        g_t, b, b_t = _gate_columns(_gate_preacts(u, wg_ref, gb_ref))
        terms = []
        for hd in heads:
            li_r = g_t[hd:hd + 1, :]
            b_r = b_t[N_HEADS + hd:N_HEADS + hd + 1, :]
            b_c = b[:, N_HEADS + hd:N_HEADS + hd + 1]
            log_d = jnp.where(causal, b_c - b_r + li_r, -jnp.inf)
            log_inter = b_c + m_run[hd]
            m_row = jnp.maximum(log_inter, jnp.max(log_d, axis=1, keepdims=True))
            m_new, decay, w_r = _state_weights(li_r, b_r, b_c[n - 1:n, :], m_run[hd])
            terms.append((jnp.exp(log_d - m_row), jnp.exp(log_inter - m_row), jnp.exp(-m_row), decay, w_r))
            m_run[hd] = m_new
        return u, terms

    def proj(u, off, width, hd):
        return jnp.dot(u, w_ref[:, off + hd * width:off + (hd + 1) * width], preferred_element_type=F32)

    def qkv_proj(u, hd):
        q = (proj(u, 0, DK, hd) * (DK ** -0.5)).astype(BF16)
        return q, _k_transposed(wkt_ref, u, hd), proj(u, V_OFF, DV, hd).astype(BF16)

    n_items = n_sub * N_HEADS
    prep = [chunk_prep(0)]
    q, kt, v = qkv_proj(prep[0][0], 0)
    for item in range(n_items):
        ci, hd = divmod(item, N_HEADS)
        u, terms = prep[ci]
        d, inter, inv_floor, decay, w_r = terms[hd]
        if hd == 0:
            acc = h_ref[0, ci * n:(ci + 1) * n, :]
        c_ext = c_ref[hd]
        sd = jnp.dot(q, kt.astype(BF16), preferred_element_type=F32)
        o_gate = jax.nn.sigmoid(proj(u, O_OFF, DV, hd)) * hw_ref[:, hd * DV:(hd + 1) * DV]
        z_pre = proj(u, Z_OFF, DV, hd)
        oz_gate = o_gate * (z_pre * jax.nn.sigmoid(z_pre))
        sd = sd * d
        intra = jnp.dot(sd.astype(BF16), v, preferred_element_type=F32)
        cross = jnp.dot(q, c_ext.astype(BF16), preferred_element_type=F32)
        if hd == 1 and ci + 1 < n_sub:
            prep.append(chunk_prep(ci + 1))
        num = intra + inter * cross[:, :DV]
        den = jnp.sum(sd, axis=1, keepdims=True) + inter * cross[:, DV:DV + 1]
        inv = 1.0 / jnp.maximum(jnp.abs(den), inv_floor)
        h_scale = inv * lax.rsqrt(inv * inv * jnp.mean(num * num, axis=-1, keepdims=True) + RMS_EPS)
        c_ref[hd] = _state_apply(c_ext, decay, w_r, kt, v)
        if item + 1 < n_items:
            ci_next, hd_next = divmod(item + 1, N_HEADS)
            q, kt, v = qkv_proj(prep[ci_next][0], hd_next)
        gated = num * oz_gate
        acc = acc + h_scale * jnp.dot(gated.astype(BF16), wo_ref[hd * DV:(hd + 1) * DV, :],
                                      preferred_element_type=F32)
        if hd + 1 == N_HEADS:
            o_ref[0, ci * n:(ci + 1) * n, :] = _rms_scale(acc, fw_ref[...])
    for hd in heads:
        m_ref[hd] = jnp.broadcast_to(m_run[hd], (SUBLANES, LANES))


def _mlstm_layer(h1, c0, m0, nw, w, wkt, wg, gb, hw, wo, fw):
    bsz, t_len, d = h1.shape
    n_l = MLSTM_TM
    return pl.pallas_call(
        _mlstm_layer_kernel,
        grid=(bsz, t_len // n_l),
        in_specs=[pl.BlockSpec((1, n_l, d), lambda b, t: (b, t, 0))]
        + [_resident(a.shape) for a in (c0, m0, nw, w, wkt, wg, gb, hw, wo, fw)],
        out_specs=pl.BlockSpec((1, n_l, d), lambda b, t: (b, t, 0)),
        out_shape=jax.ShapeDtypeStruct((bsz, t_len, d), F32),
        scratch_shapes=[pltpu.VMEM((N_HEADS, DK, DV_EXT), F32),
                        pltpu.VMEM((N_HEADS, SUBLANES, LANES), F32)],
        compiler_params=pltpu.CompilerParams(
            dimension_semantics=("arbitrary", "arbitrary"), vmem_limit_bytes=VMEM_LIMIT),
        name="mlstm_layer",
    )(h1, c0, m0, nw, w, wkt, wg, gb, hw, wo, fw)


def _meta_state_kernel(h_ref, nw_ref, wv_ref, wkt_ref, wg_ref, gb_ref, c_ref, m_ref):
    h1 = h_ref[...]
    n = h1.shape[0]
    u = _rms_scale(h1, nw_ref[...]).astype(BF16)
    g_t, b, b_t = _gate_columns(_gate_preacts(u, wg_ref, gb_ref))
    for hd in range(N_HEADS):
        v = jnp.dot(u, wv_ref[:, hd * DV:(hd + 1) * DV], preferred_element_type=F32).astype(BF16)
        b_last = b[n - 1:n, N_HEADS + hd:N_HEADS + hd + 1]
        m_new, decay, w_r = _state_weights(g_t[hd:hd + 1, :], b_t[N_HEADS + hd:N_HEADS + hd + 1, :],
                                           b_last, jnp.zeros((1, 1), F32))
        c_ref[hd] = _state_apply(jnp.zeros((DK, DV_EXT), F32), decay, w_r, _k_transposed(wkt_ref, u, hd), v)
        m_ref[hd] = jnp.broadcast_to(m_new, (SUBLANES, LANES))


def _meta_state(h1_meta, nw, w, wkt, wg, gb):
    full = lambda a: pl.BlockSpec(a.shape, lambda i: (0,) * a.ndim)
    return pl.pallas_call(
        _meta_state_kernel,
        grid=(1,),
        in_specs=[full(h1_meta), full(nw),
                  pl.BlockSpec((D_MODEL, E_MLSTM), lambda i: (0, V_OFF // E_MLSTM)),
                  full(wkt), full(wg), full(gb)],
        out_specs=[pl.BlockSpec((N_HEADS, DK, DV_EXT), lambda i: (0, 0, 0)),
                   pl.BlockSpec((N_HEADS, SUBLANES, LANES), lambda i: (0, 0, 0))],
        out_shape=[jax.ShapeDtypeStruct((N_HEADS, DK, DV_EXT), F32),
                   jax.ShapeDtypeStruct((N_HEADS, SUBLANES, LANES), F32)],
        compiler_params=pltpu.CompilerParams(
            dimension_semantics=("arbitrary",), vmem_limit_bytes=VMEM_LIMIT),
        name="mlstm_meta_state",
    )(h1_meta, nw, w, wkt, wg, gb)


def kernel(x, meta_tokens, norm_w, conv_in_w, conv_w, conv_out_w, mlstm_in_w, mlstm_gate_b,
           mlstm_head_norm_w, mlstm_out_w, final_norm_w):
    lane_pad = lambda a: jnp.pad(a.astype(BF16), ((0, 0), (0, LANE_PAD)))
    w_in = lane_pad(conv_in_w[0])
    cw = conv_w[0]
    w_out = lane_pad(conv_out_w[0])
    mw = mlstm_in_w[0]
    w = mw.astype(BF16)
    wkt = mw[:, QK:2 * QK].reshape(D_MODEL, N_HEADS, DK).transpose(1, 2, 0).astype(BF16)
    reps = 3
    wg = jnp.pad(jnp.tile(mw[:, MLSTM_PROJ_W:], (1, reps)), ((0, 0), (0, LANES - reps * N_GATES))).astype(BF16)
    gb = jnp.pad(jnp.tile(mlstm_gate_b[0], reps), (0, LANES - reps * N_GATES)).reshape(1, LANES)
    hw = mlstm_head_norm_w[0].reshape(1, E_MLSTM)
    wo = mlstm_out_w[0].astype(BF16)
    nw0 = norm_w[0].reshape(1, D_MODEL)
    nw1 = norm_w[1].reshape(1, D_MODEL)
    fw = final_norm_w.reshape(1, D_MODEL)

    zero_halo = jnp.zeros((SUBLANES, E_CONV), F32)
    h1_meta, meta_tail = _conv_layer(meta_tokens[None], zero_halo, nw0, w_in, cw, w_out,
                                     tm=N_META, emit_tail=True)
    c0, m0 = _meta_state(h1_meta[0], nw1, w, wkt, wg, gb)

    (h1,) = _conv_layer(x, meta_tail, nw0, w_in, cw, w_out, tm=CONV_TM, emit_tail=False)
    return _mlstm_layer(h1, c0, m0, nw1, w, wkt, wg, gb, hw, wo, fw)
```

```python
import functools

import jax
import jax.numpy as jnp
from jax import lax
from jax.experimental import pallas as pl
from jax.experimental.pallas import tpu as pltpu

D_MODEL = 1024
N_META = 16
E_CONV = 2048
CONV_WIDTH = 3
E_MLSTM = 2048
N_HEADS = 4
DV = E_MLSTM // N_HEADS
DK = DV // 2
QK = N_HEADS * DK
N_GATES = 2 * N_HEADS
RMS_EPS = 1e-6

LANES = 128
SUBLANES = 8
DV_EXT = DV + LANES
V_OFF = 2 * QK
O_OFF = V_OFF + E_MLSTM
Z_OFF = O_OFF + E_MLSTM
MLSTM_PROJ_W = Z_OFF + E_MLSTM

CONV_TM = 1024
CONV_TE = 1024
MLSTM_L = 256
MLSTM_TM = 512
VMEM_LIMIT = 56 * 1024 * 1024
LANE_PAD = LANES

F32 = jnp.float32
BF16 = jnp.bfloat16


def _rms_scale(x, w_row):
    ms = jnp.mean(x * x, axis=-1, keepdims=True)
    return x * lax.rsqrt(ms + RMS_EPS) * w_row


def _log_sigmoid(x):
    return jnp.minimum(x, 0.0) - jnp.log1p(jnp.exp(-jnp.abs(x)))


def _resident(shape):
    nd = len(shape)
    return pl.BlockSpec(shape, lambda *_: (0,) * nd, pipeline_mode=pl.Buffered(1))


def _conv_layer_kernel(x_ref, halo_in_ref, nw_ref, w_in_ref, cw_ref, w_out_ref, *rest,
                       te, emit_tail):
    if emit_tail:
        h_ref, tail_ref, halo_ref = rest
    else:
        h_ref, halo_ref = rest

    @pl.when(pl.program_id(1) == 0)
    def _():
        halo_ref[...] = halo_in_ref[...]

    x = x_ref[0]
    tm = x.shape[0]
    u = _rms_scale(x, nw_ref[...]).astype(BF16)
    row = lax.broadcasted_iota(jnp.int32, (tm, te), 0)
    n_chunks = E_CONV // te

    def in_proj(e):
        return [jnp.dot(u, w_in_ref[:, k * E_CONV + e * te:k * E_CONV + (e + 1) * te],
                        preferred_element_type=F32) for k in range(4)]

    acc = x
    p_next = in_proj(0)
    for e in range(n_chunks):
        lo = e * te
        b_gate, c_gate, xin, z = p_next
        if e + 1 < n_chunks:
            p_next = in_proj(e + 1)
        zb_gate = (z * jax.nn.sigmoid(z)) * b_gate
        cx = c_gate * xin
        halo = halo_ref[:, lo:lo + te]
        prev1 = jnp.where(row == 0, halo[7:8, :], pltpu.roll(cx, 1, 0))
        prev2 = jnp.where(row == 0, halo[6:7, :],
                          jnp.where(row == 1, halo[7:8, :], pltpu.roll(cx, 2, 0)))
        cw = cw_ref[:, lo:lo + te]
        y = cw[0:1, :] * prev2 + cw[1:2, :] * prev1 + cw[2:3, :] * cx
        halo_ref[:, lo:lo + te] = cx[tm - SUBLANES:, :]
        g = zb_gate * y
        acc = acc + jnp.dot(g.astype(BF16), w_out_ref[lo:lo + te, :D_MODEL], preferred_element_type=F32)
    h_ref[0] = acc
    if emit_tail:
        tail_ref[...] = halo_ref[...]


def _conv_layer(x, halo_in, nw, w_in, cw, w_out, *, tm, emit_tail):
    bsz, t_len, d = x.shape
    out_shape = [jax.ShapeDtypeStruct((bsz, t_len, d), F32)]
    out_specs = [pl.BlockSpec((1, tm, d), lambda b, t: (b, t, 0))]
    if emit_tail:
        out_shape.append(jax.ShapeDtypeStruct((SUBLANES, E_CONV), F32))
        out_specs.append(pl.BlockSpec((SUBLANES, E_CONV), lambda b, t: (0, 0)))
    return pl.pallas_call(
        functools.partial(_conv_layer_kernel, te=min(CONV_TE, E_CONV), emit_tail=emit_tail),
        grid=(bsz, t_len // tm),
        in_specs=[
            pl.BlockSpec((1, tm, d), lambda b, t: (b, t, 0)),
            _resident(halo_in.shape),
            _resident(nw.shape),
            _resident(w_in.shape),
            _resident(cw.shape),
            _resident(w_out.shape),
        ],
        out_specs=out_specs,
        out_shape=out_shape,
        scratch_shapes=[pltpu.VMEM((SUBLANES, E_CONV), F32)],
        compiler_params=pltpu.CompilerParams(
            dimension_semantics=("arbitrary", "arbitrary"), vmem_limit_bytes=VMEM_LIMIT),
        name="conv_layer_meta" if emit_tail else "conv_layer",
    )(x, halo_in, nw, w_in, cw, w_out)


def _gate_preacts(u, wg_ref, gb_ref):
    return jnp.dot(u, wg_ref[...], preferred_element_type=F32) + gb_ref[...]


def _gate_columns(g):
    n = g.shape[0]
    lf = _log_sigmoid(g)
    hi = lf.astype(BF16).astype(F32)
    mid = (lf - hi).astype(BF16).astype(F32)
    low = lf - hi - mid
    lane = lax.broadcasted_iota(jnp.int32, (n, LANES), 1)
    terms = jnp.where(lane < N_GATES, hi, jnp.where(lane < 2 * N_GATES, mid, low)).astype(BF16)
    r = lax.broadcasted_iota(jnp.int32, (n, n), 0)
    c = lax.broadcasted_iota(jnp.int32, (n, n), 1)
    tril = (r >= c).astype(BF16)
    part = jnp.dot(tril, terms, preferred_element_type=F32)
    b = part + pltpu.roll(part, LANES - N_GATES, 1) + pltpu.roll(part, LANES - 2 * N_GATES, 1)
    return g.T, b, b.T


def _state_weights(li_r, b_r, b_last, m_prev):
    logw = b_last - b_r + li_r
    m_new = jnp.maximum(b_last + m_prev, jnp.max(logw, axis=1, keepdims=True))
    return m_new, jnp.exp(b_last + m_prev - m_new), jnp.exp(logw - m_new)


def _state_apply(c_ext, decay, w_r, kt, v):
    kw = kt * w_r
    c_new = decay * c_ext[:, :DV] + jnp.dot(kw.astype(BF16), v, preferred_element_type=F32)
    n_new = decay * c_ext[:, DV:DV + 1] + jnp.sum(kw, axis=1, keepdims=True)
    lane = lax.broadcasted_iota(jnp.int32, (DK, LANES), 1)
    return jnp.concatenate([c_new, jnp.where(lane == 0, n_new, 0.0)], axis=1)


def _k_transposed(wkt_ref, u, hd):
    return lax.dot_general(wkt_ref[hd], u, (((1,), (1,)), ((), ())), preferred_element_type=F32)


def _mlstm_layer_kernel(h_ref, c0_ref, m0_ref, nw_ref, w_ref, wkt_ref, wg_ref, gb_ref, hw_ref,
                        wo_ref, fw_ref, o_ref, c_ref, m_ref):
    @pl.when(pl.program_id(1) == 0)
    def _():
        c_ref[...] = c0_ref[...]
        m_ref[...] = m0_ref[...]

    n = MLSTM_L
    n_sub = h_ref.shape[1] // n
    r = lax.broadcasted_iota(jnp.int32, (n, n), 0)
    c = lax.broadcasted_iota(jnp.int32, (n, n), 1)
    causal = r >= c
    heads = range(N_HEADS)
    m_run = [m_ref[hd][0:1, 0:1] for hd in heads]

    def chunk_prep(ci):
        x = h_ref[0, ci * n:(ci + 1) * n, :]
        if ci == 0:
            xw = x * nw_ref[...]
            rs = lax.rsqrt(jnp.mean(x * x, axis=-1, keepdims=True) + RMS_EPS)
            g = jnp.dot(xw.astype(BF16), wg_ref[...], preferred_element_type=F32) * rs + gb_ref[...]
            u = (xw * rs).astype(BF16)
        else:
            u = _rms_scale(x, nw_ref[...]).astype(BF16)
            g = _gate_preacts(u, wg_ref, gb_ref)
        g_t, b, b_t = _gate_columns(g)
        terms = []
        for hd in heads:
            li_r = g_t[hd:hd + 1, :]
            b_r = b_t[N_HEADS + hd:N_HEADS + hd + 1, :]
            b_c = b[:, N_HEADS + hd:N_HEADS + hd + 1]
            log_d = jnp.where(causal, b_c - b_r + li_r, -jnp.inf)
            log_inter = b_c + m_run[hd]
            m_row = jnp.maximum(log_inter, jnp.max(log_d, axis=1, keepdims=True))
            m_new, decay, w_r = _state_weights(li_r, b_r, b_c[n - 1:n, :], m_run[hd])
            terms.append((jnp.exp(log_d - m_row), jnp.exp(log_inter - m_row), jnp.exp(-m_row), decay, w_r))
            m_run[hd] = m_new
        return u, terms

    def proj(u, off, width, hd):
        return jnp.dot(u, w_ref[:, off + hd * width:off + (hd + 1) * width], preferred_element_type=F32)

    def qkv_proj(u, hd):
        q = (proj(u, 0, DK, hd) * (DK ** -0.5)).astype(BF16)
        return q, _k_transposed(wkt_ref, u, hd), proj(u, V_OFF, DV, hd).astype(BF16)

    n_items = n_sub * N_HEADS
    prep = [chunk_prep(0)]
    q, kt, v = qkv_proj(prep[0][0], 0)
    for item in range(n_items):
        ci, hd = divmod(item, N_HEADS)
        u, terms = prep[ci]
        d, inter, inv_floor, decay, w_r = terms[hd]
        if hd == 0:
            acc = h_ref[0, ci * n:(ci + 1) * n, :]
        c_ext = c_ref[hd]
        sd = jnp.dot(q, kt.astype(BF16), preferred_element_type=F32)
        o_gate = jax.nn.sigmoid(proj(u, O_OFF, DV, hd)) * hw_ref[:, hd * DV:(hd + 1) * DV]
        z_pre = proj(u, Z_OFF, DV, hd)
        oz_gate = o_gate * (z_pre * jax.nn.sigmoid(z_pre))
        sd = sd * d
        intra = jnp.dot(sd.astype(BF16), v, preferred_element_type=F32)
        cross = jnp.dot(q, c_ext.astype(BF16), preferred_element_type=F32)
        if hd == 1 and ci + 1 < n_sub:
            prep.append(chunk_prep(ci + 1))
        num = intra + inter * cross[:, :DV]
        den = jnp.sum(sd, axis=1, keepdims=True) + inter * cross[:, DV:DV + 1]
        inv = 1.0 / jnp.maximum(jnp.abs(den), inv_floor)
        h_scale = inv * lax.rsqrt(inv * inv * jnp.mean(num * num, axis=-1, keepdims=True) + RMS_EPS)
        c_ref[hd] = _state_apply(c_ext, decay, w_r, kt, v)
        if item + 1 < n_items:
            ci_next, hd_next = divmod(item + 1, N_HEADS)
            q, kt, v = qkv_proj(prep[ci_next][0], hd_next)
        gated = num * oz_gate
        acc = acc + h_scale * jnp.dot(gated.astype(BF16), wo_ref[hd * DV:(hd + 1) * DV, :],
                                      preferred_element_type=F32)
        if hd + 1 == N_HEADS:
            o_ref[0, ci * n:(ci + 1) * n, :] = _rms_scale(acc, fw_ref[...])
    for hd in heads:
        m_ref[hd] = jnp.broadcast_to(m_run[hd], (SUBLANES, LANES))


def _mlstm_layer(h1, c0, m0, nw, w, wkt, wg, gb, hw, wo, fw):
    bsz, t_len, d = h1.shape
    n_l = MLSTM_TM
    return pl.pallas_call(
        _mlstm_layer_kernel,
        grid=(bsz, t_len // n_l),
        in_specs=[pl.BlockSpec((1, n_l, d), lambda b, t: (b, t, 0))]
        + [_resident(a.shape) for a in (c0, m0, nw, w, wkt, wg, gb, hw, wo, fw)],
        out_specs=pl.BlockSpec((1, n_l, d), lambda b, t: (b, t, 0)),
        out_shape=jax.ShapeDtypeStruct((bsz, t_len, d), F32),
        scratch_shapes=[pltpu.VMEM((N_HEADS, DK, DV_EXT), F32),
                        pltpu.VMEM((N_HEADS, SUBLANES, LANES), F32)],
        compiler_params=pltpu.CompilerParams(
            dimension_semantics=("arbitrary", "arbitrary"), vmem_limit_bytes=VMEM_LIMIT),
        name="mlstm_layer",
    )(h1, c0, m0, nw, w, wkt, wg, gb, hw, wo, fw)


def _meta_state_kernel(h_ref, nw_ref, wv_ref, wkt_ref, wg_ref, gb_ref, c_ref, m_ref):
    h1 = h_ref[...]
    n = h1.shape[0]
    u = _rms_scale(h1, nw_ref[...]).astype(BF16)
    g_t, b, b_t = _gate_columns(_gate_preacts(u, wg_ref, gb_ref))
    for hd in range(N_HEADS):
        v = jnp.dot(u, wv_ref[:, hd * DV:(hd + 1) * DV], preferred_element_type=F32).astype(BF16)
        b_last = b[n - 1:n, N_HEADS + hd:N_HEADS + hd + 1]
        m_new, decay, w_r = _state_weights(g_t[hd:hd + 1, :], b_t[N_HEADS + hd:N_HEADS + hd + 1, :],
                                           b_last, jnp.zeros((1, 1), F32))
        c_ref[hd] = _state_apply(jnp.zeros((DK, DV_EXT), F32), decay, w_r, _k_transposed(wkt_ref, u, hd), v)
        m_ref[hd] = jnp.broadcast_to(m_new, (SUBLANES, LANES))


def _meta_state(h1_meta, nw, w, wkt, wg, gb):
    full = lambda a: pl.BlockSpec(a.shape, lambda i: (0,) * a.ndim)
    return pl.pallas_call(
        _meta_state_kernel,
        grid=(1,),
        in_specs=[full(h1_meta), full(nw),
                  pl.BlockSpec((D_MODEL, E_MLSTM), lambda i: (0, V_OFF // E_MLSTM)),
                  full(wkt), full(wg), full(gb)],
        out_specs=[pl.BlockSpec((N_HEADS, DK, DV_EXT), lambda i: (0, 0, 0)),
                   pl.BlockSpec((N_HEADS, SUBLANES, LANES), lambda i: (0, 0, 0))],
        out_shape=[jax.ShapeDtypeStruct((N_HEADS, DK, DV_EXT), F32),
                   jax.ShapeDtypeStruct((N_HEADS, SUBLANES, LANES), F32)],
        compiler_params=pltpu.CompilerParams(
            dimension_semantics=("arbitrary",), vmem_limit_bytes=VMEM_LIMIT),
        name="mlstm_meta_state",
    )(h1_meta, nw, w, wkt, wg, gb)


def kernel(x, meta_tokens, norm_w, conv_in_w, conv_w, conv_out_w, mlstm_in_w, mlstm_gate_b,
           mlstm_head_norm_w, mlstm_out_w, final_norm_w):
    lane_pad = lambda a: jnp.pad(a.astype(BF16), ((0, 0), (0, LANE_PAD)))
    w_in = lane_pad(conv_in_w[0])
    cw = conv_w[0]
    w_out = lane_pad(conv_out_w[0])
    mw = mlstm_in_w[0]
    w = mw.astype(BF16)
    wkt = mw[:, QK:2 * QK].reshape(D_MODEL, N_HEADS, DK).transpose(1, 2, 0).astype(BF16)
    reps = 3
    wg = jnp.pad(jnp.tile(mw[:, MLSTM_PROJ_W:], (1, reps)), ((0, 0), (0, LANES - reps * N_GATES))).astype(BF16)
    gb = jnp.pad(jnp.tile(mlstm_gate_b[0], reps), (0, LANES - reps * N_GATES)).reshape(1, LANES)
    hw = mlstm_head_norm_w[0].reshape(1, E_MLSTM)
    wo = mlstm_out_w[0].astype(BF16)
    nw0 = norm_w[0].reshape(1, D_MODEL)
    nw1 = norm_w[1].reshape(1, D_MODEL)
    fw = final_norm_w.reshape(1, D_MODEL)

    zero_halo = jnp.zeros((SUBLANES, E_CONV), F32)
    h1_meta, meta_tail = _conv_layer(meta_tokens[None], zero_halo, nw0, w_in, cw, w_out,
                                     tm=N_META, emit_tail=True)
    c0, m0 = _meta_state(h1_meta[0], nw1, w, wkt, wg, gb)

    (h1,) = _conv_layer(x, meta_tail, nw0, w_in, cw, w_out, tm=CONV_TM, emit_tail=False)
    return _mlstm_layer(h1, c0, m0, nw1, w, wkt, wg, gb, hw, wo, fw)
```

```python
import functools

import jax
import jax.numpy as jnp
from jax import lax
from jax.experimental import pallas as pl
from jax.experimental.pallas import tpu as pltpu

D_MODEL = 1024
N_META = 16
E_CONV = 2048
CONV_WIDTH = 3
E_MLSTM = 2048
N_HEADS = 4
DV = E_MLSTM // N_HEADS
DK = DV // 2
QK = N_HEADS * DK
N_GATES = 2 * N_HEADS
RMS_EPS = 1e-6

LANES = 128
SUBLANES = 8
DV_EXT = DV + LANES
V_OFF = 2 * QK
O_OFF = V_OFF + E_MLSTM
Z_OFF = O_OFF + E_MLSTM
MLSTM_PROJ_W = Z_OFF + E_MLSTM

CONV_TM = 1024
CONV_TE = 1024
MLSTM_L = 256
MLSTM_TM = 512
VMEM_LIMIT = 56 * 1024 * 1024
LANE_PAD = LANES

F32 = jnp.float32
BF16 = jnp.bfloat16


def _rms_scale(x, w_row):
    ms = jnp.mean(x * x, axis=-1, keepdims=True)
    return x * lax.rsqrt(ms + RMS_EPS) * w_row


def _log_sigmoid(x):
    return jnp.minimum(x, 0.0) - jnp.log1p(jnp.exp(-jnp.abs(x)))


def _resident(shape):
    nd = len(shape)
    return pl.BlockSpec(shape, lambda *_: (0,) * nd, pipeline_mode=pl.Buffered(1))


def _conv_layer_kernel(x_ref, halo_in_ref, nw_ref, w_in_ref, cw_ref, w_out_ref, *rest,
                       te, emit_tail):
    if emit_tail:
        h_ref, tail_ref, halo_ref = rest
    else:
        h_ref, halo_ref = rest

    @pl.when(pl.program_id(1) == 0)
    def _():
        halo_ref[...] = halo_in_ref[...]

    x = x_ref[0]
    tm = x.shape[0]
    u = _rms_scale(x, nw_ref[...]).astype(BF16)
    row = lax.broadcasted_iota(jnp.int32, (tm, te), 0)
    n_chunks = E_CONV // te

    def in_proj(e):
        return [jnp.dot(u, w_in_ref[:, k * E_CONV + e * te:k * E_CONV + (e + 1) * te],
                        preferred_element_type=F32) for k in range(4)]

    acc = x
    p_next = in_proj(0)
    for e in range(n_chunks):
        lo = e * te
        b_gate, c_gate, xin, z = p_next
        if e + 1 < n_chunks:
            p_next = in_proj(e + 1)
        zb_gate = (z * jax.nn.sigmoid(z)) * b_gate
        cx = c_gate * xin
        halo = halo_ref[:, lo:lo + te]
        prev1 = jnp.where(row == 0, halo[7:8, :], pltpu.roll(cx, 1, 0))
        prev2 = jnp.where(row == 0, halo[6:7, :],
                          jnp.where(row == 1, halo[7:8, :], pltpu.roll(cx, 2, 0)))
        cw = cw_ref[:, lo:lo + te]
        y = cw[0:1, :] * prev2 + cw[1:2, :] * prev1 + cw[2:3, :] * cx
        halo_ref[:, lo:lo + te] = cx[tm - SUBLANES:, :]
        g = zb_gate * y
        acc = acc + jnp.dot(g.astype(BF16), w_out_ref[lo:lo + te, :D_MODEL], preferred_element_type=F32)
    h_ref[0] = acc
    if emit_tail:
        tail_ref[...] = halo_ref[...]


def _conv_layer(x, halo_in, nw, w_in, cw, w_out, *, tm, emit_tail):
    bsz, t_len, d = x.shape
    out_shape = [jax.ShapeDtypeStruct((bsz, t_len, d), F32)]
    out_specs = [pl.BlockSpec((1, tm, d), lambda b, t: (b, t, 0))]
    if emit_tail:
        out_shape.append(jax.ShapeDtypeStruct((SUBLANES, E_CONV), F32))
        out_specs.append(pl.BlockSpec((SUBLANES, E_CONV), lambda b, t: (0, 0)))
    return pl.pallas_call(
        functools.partial(_conv_layer_kernel, te=min(CONV_TE, E_CONV), emit_tail=emit_tail),
        grid=(bsz, t_len // tm),
        in_specs=[
            pl.BlockSpec((1, tm, d), lambda b, t: (b, t, 0)),
            _resident(halo_in.shape),
            _resident(nw.shape),
            _resident(w_in.shape),
            _resident(cw.shape),
            _resident(w_out.shape),
        ],
        out_specs=out_specs,
        out_shape=out_shape,
        scratch_shapes=[pltpu.VMEM((SUBLANES, E_CONV), F32)],
        compiler_params=pltpu.CompilerParams(
            dimension_semantics=("arbitrary", "arbitrary"), vmem_limit_bytes=VMEM_LIMIT),
        name="conv_layer_meta" if emit_tail else "conv_layer",
    )(x, halo_in, nw, w_in, cw, w_out)


def _gate_preacts(u, wg_ref, gb_ref):
    return jnp.dot(u, wg_ref[...], preferred_element_type=F32) + gb_ref[...]


def _gate_columns(g):
    n = g.shape[0]
    lf = _log_sigmoid(g)
    hi = lf.astype(BF16).astype(F32)
    mid = (lf - hi).astype(BF16).astype(F32)
    low = lf - hi - mid
    lane = lax.broadcasted_iota(jnp.int32, (n, LANES), 1)
    terms = jnp.where(lane < N_GATES, hi, jnp.where(lane < 2 * N_GATES, mid, low)).astype(BF16)
    r = lax.broadcasted_iota(jnp.int32, (n, n), 0)
    c = lax.broadcasted_iota(jnp.int32, (n, n), 1)
    tril = (r >= c).astype(BF16)
    part = jnp.dot(tril, terms, preferred_element_type=F32)
    b = part + pltpu.roll(part, LANES - N_GATES, 1) + pltpu.roll(part, LANES - 2 * N_GATES, 1)
    return g.T, b, b.T


def _state_weights(li_r, b_r, b_last, m_prev):
    logw = b_last - b_r + li_r
    m_new = jnp.maximum(b_last + m_prev, jnp.max(logw, axis=1, keepdims=True))
    return m_new, jnp.exp(b_last + m_prev - m_new), jnp.exp(logw - m_new)


def _state_apply(c_ext, decay, w_r, kt, v):
    kw = kt * w_r
    c_new = decay * c_ext[:, :DV] + jnp.dot(kw.astype(BF16), v, preferred_element_type=F32)
    n_new = decay * c_ext[:, DV:DV + 1] + jnp.sum(kw, axis=1, keepdims=True)
    lane = lax.broadcasted_iota(jnp.int32, (DK, LANES), 1)
    return jnp.concatenate([c_new, jnp.where(lane == 0, n_new, 0.0)], axis=1)


def _k_transposed(wkt_ref, u, hd):
    return lax.dot_general(wkt_ref[hd], u, (((1,), (1,)), ((), ())), preferred_element_type=F32)


def _mlstm_layer_kernel(h_ref, c0_ref, m0_ref, nw_ref, w_ref, wkt_ref, wg_ref, gb_ref, hw_ref,
                        wo_ref, fw_ref, o_ref, c_ref, m_ref):
    @pl.when(pl.program_id(1) == 0)
    def _():
        c_ref[...] = c0_ref[...]
        m_ref[...] = m0_ref[...]

    n = MLSTM_L
    n_sub = h_ref.shape[1] // n
    r = lax.broadcasted_iota(jnp.int32, (n, n), 0)
    c = lax.broadcasted_iota(jnp.int32, (n, n), 1)
    causal = r >= c
    heads = range(N_HEADS)
    m_run = [m_ref[hd][0:1, 0:1] for hd in heads]

    def chunk_prep(ci):
        x = h_ref[0, ci * n:(ci + 1) * n, :]
        if ci == 0:
            xw = x * nw_ref[...]
            rs = lax.rsqrt(jnp.mean(x * x, axis=-1, keepdims=True) + RMS_EPS)
            g = jnp.dot(xw.astype(BF16), wg_ref[...], preferred_element_type=F32) * rs + gb_ref[...]
            u = (xw * rs).astype(BF16)
        else:
            u = _rms_scale(x, nw_ref[...]).astype(BF16)
            g = _gate_preacts(u, wg_ref, gb_ref)
        g_t, b, b_t = _gate_columns(g)
        terms = []
        for hd in heads:
            li_r = g_t[hd:hd + 1, :]
            b_r = b_t[N_HEADS + hd:N_HEADS + hd + 1, :]
            b_c = b[:, N_HEADS + hd:N_HEADS + hd + 1]
            log_d = jnp.where(causal, b_c - b_r + li_r, -jnp.inf)
            log_inter = b_c + m_run[hd]
            m_row = jnp.maximum(log_inter, jnp.max(log_d, axis=1, keepdims=True))
            m_new, decay, w_r = _state_weights(li_r, b_r, b_c[n - 1:n, :], m_run[hd])
            terms.append((jnp.exp(log_d - m_row), jnp.exp(log_inter - m_row), jnp.exp(-m_row), decay, w_r))
            m_run[hd] = m_new
        return u, terms

    def proj(u, off, width, hd):
        return jnp.dot(u, w_ref[:, off + hd * width:off + (hd + 1) * width], preferred_element_type=F32)

    def qkv_proj(u, hd):
        q = (proj(u, 0, DK, hd) * (DK ** -0.5)).astype(BF16)
        return q, _k_transposed(wkt_ref, u, hd), proj(u, V_OFF, DV, hd).astype(BF16)

    n_items = n_sub * N_HEADS
    prep = [chunk_prep(0)]
    q, kt, v = qkv_proj(prep[0][0], 0)
    for item in range(n_items):
        ci, hd = divmod(item, N_HEADS)
        u, terms = prep[ci]
        d, inter, inv_floor, decay, w_r = terms[hd]
        if hd == 0:
            acc = h_ref[0, ci * n:(ci + 1) * n, :]
        c_ext = c_ref[hd]
        sd = jnp.dot(q, kt.astype(BF16), preferred_element_type=F32)
        o_gate = jax.nn.sigmoid(proj(u, O_OFF, DV, hd)) * hw_ref[:, hd * DV:(hd + 1) * DV]
        z_pre = proj(u, Z_OFF, DV, hd)
        oz_gate = o_gate * (z_pre * jax.nn.sigmoid(z_pre))
        sd = sd * d
        intra = jnp.dot(sd.astype(BF16), v, preferred_element_type=F32)
        cross = jnp.dot(q, c_ext.astype(BF16), preferred_element_type=F32)
        if hd == 1 and ci + 1 < n_sub:
            prep.append(chunk_prep(ci + 1))
        num = intra + inter * cross[:, :DV]
        den = jnp.sum(sd, axis=1, keepdims=True) + inter * cross[:, DV:DV + 1]
        inv = 1.0 / jnp.maximum(jnp.abs(den), inv_floor)
        h_scale = inv * lax.rsqrt(inv * inv * jnp.mean(num * num, axis=-1, keepdims=True) + RMS_EPS)
        c_ref[hd] = _state_apply(c_ext, decay, w_r, kt, v)
        if item + 1 < n_items:
            ci_next, hd_next = divmod(item + 1, N_HEADS)
            q, kt, v = qkv_proj(prep[ci_next][0], hd_next)
        if hd == 0:
            gated = []
        gated.append((num * (h_scale * oz_gate)).astype(BF16))
        if hd + 1 == N_HEADS:
            acc = acc + jnp.dot(jnp.concatenate(gated, axis=1), wo_ref[...], preferred_element_type=F32)
            o_ref[0, ci * n:(ci + 1) * n, :] = _rms_scale(acc, fw_ref[...])
    for hd in heads:
        m_ref[hd] = jnp.broadcast_to(m_run[hd], (SUBLANES, LANES))


def _mlstm_layer(h1, c0, m0, nw, w, wkt, wg, gb, hw, wo, fw):
    bsz, t_len, d = h1.shape
    n_l = MLSTM_TM
    return pl.pallas_call(
        _mlstm_layer_kernel,
        grid=(bsz, t_len // n_l),
        in_specs=[pl.BlockSpec((1, n_l, d), lambda b, t: (b, t, 0))]
        + [_resident(a.shape) for a in (c0, m0, nw, w, wkt, wg, gb, hw, wo, fw)],
        out_specs=pl.BlockSpec((1, n_l, d), lambda b, t: (b, t, 0)),
        out_shape=jax.ShapeDtypeStruct((bsz, t_len, d), F32),
        scratch_shapes=[pltpu.VMEM((N_HEADS, DK, DV_EXT), F32),
                        pltpu.VMEM((N_HEADS, SUBLANES, LANES), F32)],
        compiler_params=pltpu.CompilerParams(
            dimension_semantics=("arbitrary", "arbitrary"), vmem_limit_bytes=VMEM_LIMIT),
        name="mlstm_layer",
    )(h1, c0, m0, nw, w, wkt, wg, gb, hw, wo, fw)


def _meta_state_kernel(h_ref, nw_ref, wv_ref, wkt_ref, wg_ref, gb_ref, c_ref, m_ref):
    h1 = h_ref[...]
    n = h1.shape[0]
    u = _rms_scale(h1, nw_ref[...]).astype(BF16)
    g_t, b, b_t = _gate_columns(_gate_preacts(u, wg_ref, gb_ref))
    for hd in range(N_HEADS):
        v = jnp.dot(u, wv_ref[:, hd * DV:(hd + 1) * DV], preferred_element_type=F32).astype(BF16)
        b_last = b[n - 1:n, N_HEADS + hd:N_HEADS + hd + 1]
        m_new, decay, w_r = _state_weights(g_t[hd:hd + 1, :], b_t[N_HEADS + hd:N_HEADS + hd + 1, :],
                                           b_last, jnp.zeros((1, 1), F32))
        c_ref[hd] = _state_apply(jnp.zeros((DK, DV_EXT), F32), decay, w_r, _k_transposed(wkt_ref, u, hd), v)
        m_ref[hd] = jnp.broadcast_to(m_new, (SUBLANES, LANES))


def _meta_state(h1_meta, nw, w, wkt, wg, gb):
    full = lambda a: pl.BlockSpec(a.shape, lambda i: (0,) * a.ndim)
    return pl.pallas_call(
        _meta_state_kernel,
        grid=(1,),
        in_specs=[full(h1_meta), full(nw),
                  pl.BlockSpec((D_MODEL, E_MLSTM), lambda i: (0, V_OFF // E_MLSTM)),
                  full(wkt), full(wg), full(gb)],
        out_specs=[pl.BlockSpec((N_HEADS, DK, DV_EXT), lambda i: (0, 0, 0)),
                   pl.BlockSpec((N_HEADS, SUBLANES, LANES), lambda i: (0, 0, 0))],
        out_shape=[jax.ShapeDtypeStruct((N_HEADS, DK, DV_EXT), F32),
                   jax.ShapeDtypeStruct((N_HEADS, SUBLANES, LANES), F32)],
        compiler_params=pltpu.CompilerParams(
            dimension_semantics=("arbitrary",), vmem_limit_bytes=VMEM_LIMIT),
        name="mlstm_meta_state",
    )(h1_meta, nw, w, wkt, wg, gb)


def kernel(x, meta_tokens, norm_w, conv_in_w, conv_w, conv_out_w, mlstm_in_w, mlstm_gate_b,
           mlstm_head_norm_w, mlstm_out_w, final_norm_w):
    lane_pad = lambda a: jnp.pad(a.astype(BF16), ((0, 0), (0, LANE_PAD)))
    w_in = lane_pad(conv_in_w[0])
    cw = conv_w[0]
    w_out = lane_pad(conv_out_w[0])
    mw = mlstm_in_w[0]
    w = mw.astype(BF16)
    wkt = mw[:, QK:2 * QK].reshape(D_MODEL, N_HEADS, DK).transpose(1, 2, 0).astype(BF16)
    reps = 3
    wg = jnp.pad(jnp.tile(mw[:, MLSTM_PROJ_W:], (1, reps)), ((0, 0), (0, LANES - reps * N_GATES))).astype(BF16)
    gb = jnp.pad(jnp.tile(mlstm_gate_b[0], reps), (0, LANES - reps * N_GATES)).reshape(1, LANES)
    hw = mlstm_head_norm_w[0].reshape(1, E_MLSTM)
    wo = mlstm_out_w[0].astype(BF16)
    nw0 = norm_w[0].reshape(1, D_MODEL)
    nw1 = norm_w[1].reshape(1, D_MODEL)
    fw = final_norm_w.reshape(1, D_MODEL)

    zero_halo = jnp.zeros((SUBLANES, E_CONV), F32)
    h1_meta, meta_tail = _conv_layer(meta_tokens[None], zero_halo, nw0, w_in, cw, w_out,
                                     tm=N_META, emit_tail=True)
    c0, m0 = _meta_state(h1_meta[0], nw1, w, wkt, wg, gb)

    (h1,) = _conv_layer(x, meta_tail, nw0, w_in, cw, w_out, tm=CONV_TM, emit_tail=False)
    return _mlstm_layer(h1, c0, m0, nw1, w, wkt, wg, gb, hw, wo, fw)
```
